```python
import jax, jax.numpy as jnp
from jax import lax
import numpy as np

D_MODEL = 1024
BATCH = 16
SEQ = 4096
DEPTH = 1

CHUNK = 64
N_PAST_CHUNKS = 8
BAND_CHUNKS = N_PAST_CHUNKS + 1
D_RNN = 1024
RNN_BLOCKS = 8
RNN_BLOCK_DIM = D_RNN // RNN_BLOCKS
CONV_WIDTH = 4
LRU_C = 8.0
ATT_HEADS = 8
ATT_HEAD_DIM = 128
D_ATT = ATT_HEADS * ATT_HEAD_DIM
MAX_REL = 256
MEM_TOKENS = 256
MEM_HEADS = 4
MEM_HEAD_DIM = 256
D_MEM = MEM_HEADS * MEM_HEAD_DIM
N_BRANCHES = 3
D_IN = 2 * D_RNN + 4 * D_ATT + 2 * D_MEM + N_BRANCHES * D_MODEL
EPS = 1e-6
NEG_INF = -1e30

kernel_name = "hybrid_rglru_chunkattn_memxattn_gated"


def _rmsnorm(x, g):
    xf = x.astype(jnp.float32)
    y = xf * lax.rsqrt(jnp.mean(xf * xf, axis=-1, keepdims=True) + EPS)
    return (y * g.astype(jnp.float32)).astype(x.dtype)


def _split_columns(u):
    sizes = (D_RNN, D_RNN, D_ATT, D_ATT, D_ATT, D_ATT, D_MEM, D_MEM, N_BRANCHES * D_MODEL)
    points = [int(p) for p in np.cumsum(sizes)[:-1]]
    return jnp.split(u, points, axis=-1)


def _rglru_branch(xr, conv_w, conv_b, wa, ba, wx, bx, lam):
    B, S, _ = xr.shape
    xc = lax.conv_general_dilated(
        xr, conv_w[:, None, :], window_strides=(1,), padding=[(CONV_WIDTH - 1, 0)],
        dimension_numbers=("NWC", "WIO", "NWC"), feature_group_count=D_RNN) + conv_b
    xb = xc.reshape(B, S, RNN_BLOCKS, RNN_BLOCK_DIM)
    r = jax.nn.sigmoid(jnp.einsum("bsni,nij->bsnj", xb, wa).reshape(B, S, D_RNN) + ba)
    i = jax.nn.sigmoid(jnp.einsum("bsni,nij->bsnj", xb, wx).reshape(B, S, D_RNN) + bx)
    log_a = -LRU_C * r.astype(jnp.float32) * jax.nn.softplus(-lam.astype(jnp.float32))
    a = jnp.exp(log_a)
    b = jnp.sqrt(-jnp.expm1(2.0 * log_a)) * (i * xc).astype(jnp.float32)

    def combine(left, right):
        a1, b1 = left
        a2, b2 = right
        return a1 * a2, a2 * b1 + b2

    _, h = lax.associative_scan(combine, (a, b), axis=1)
    return h.astype(xr.dtype)


def _chunk_band_attention(q, k, v, q_norm_g, k_norm_g, rel_bias):
    B, S, _ = q.shape
    n_chunks = S // CHUNK
    past = N_PAST_CHUNKS * CHUNK
    band = BAND_CHUNKS * CHUNK
    q = _rmsnorm(q.reshape(B, S, ATT_HEADS, ATT_HEAD_DIM), q_norm_g)
    k = _rmsnorm(k.reshape(B, S, ATT_HEADS, ATT_HEAD_DIM), k_norm_g)
    v = v.reshape(B, S, ATT_HEADS, ATT_HEAD_DIM)
    kp = jnp.pad(k, ((0, 0), (past, 0), (0, 0), (0, 0)))
    vp = jnp.pad(v, ((0, 0), (past, 0), (0, 0), (0, 0)))
    dist = jnp.arange(CHUNK)[:, None] + past - jnp.arange(band)[None, :]
    bias = rel_bias[:, jnp.clip(dist, -MAX_REL, MAX_REL) + MAX_REL].astype(jnp.float32)
    scale = ATT_HEAD_DIM ** -0.5

    def one_chunk(c):
        start = c * CHUNK
        qc = lax.dynamic_slice_in_dim(q, start, CHUNK, axis=1)
        kc = lax.dynamic_slice_in_dim(kp, start, band, axis=1)
        vc = lax.dynamic_slice_in_dim(vp, start, band, axis=1)
        s = jnp.einsum("bqhd,bkhd->bhqk", qc, kc).astype(jnp.float32) * scale + bias
        kpos = start - past + jnp.arange(band)
        s = jnp.where(kpos[None, None, None, :] >= 0, s, NEG_INF)
        p = jax.nn.softmax(s, axis=-1).astype(vc.dtype)
        return jnp.einsum("bhqk,bkhd->bqhd", p, vc)

    o = lax.map(one_chunk, jnp.arange(n_chunks))
    return o.transpose(1, 0, 2, 3, 4).reshape(B, S, D_ATT)


def _memory_attention(qm, mem_n, w_mem_kv, q_norm_g, k_norm_g):
    B, S, _ = qm.shape
    M = mem_n.shape[1]
    q = _rmsnorm(qm.reshape(B, S, MEM_HEADS, MEM_HEAD_DIM), q_norm_g)
    km, vm = jnp.split(mem_n @ w_mem_kv, 2, axis=-1)
    k = _rmsnorm(km.reshape(B, M, MEM_HEADS, MEM_HEAD_DIM), k_norm_g)
    v = vm.reshape(B, M, MEM_HEADS, MEM_HEAD_DIM)
    s = jnp.einsum("bshd,bmhd->bhsm", q, k).astype(jnp.float32) * (MEM_HEAD_DIM ** -0.5)
    p = jax.nn.softmax(s, axis=-1).astype(v.dtype)
    return jnp.einsum("bhsm,bmhd->bshd", p, v).reshape(B, S, D_MEM)


def _layer(x, mem, norm_g, mem_norm_g, w_in, b_merge, conv_w, conv_b, lru_wa, lru_ba,
           lru_wx, lru_bx, lru_lambda, q_norm_g, k_norm_g, rel_bias, w_mem_kv,
           mem_q_norm_g, mem_k_norm_g, w_proj_rnn, w_proj_att, w_proj_mem, w_out):
    B, S, _ = x.shape
    h = _rmsnorm(x, norm_g)
    xr, gr, q, k, v, ga, qm, gm, gmerge = _split_columns(h @ w_in)
    y_rnn = (_rglru_branch(xr, conv_w, conv_b, lru_wa, lru_ba, lru_wx, lru_bx, lru_lambda)
             * jax.nn.silu(gr)) @ w_proj_rnn
    y_att = (_chunk_band_attention(q, k, v, q_norm_g, k_norm_g, rel_bias)
             * jax.nn.silu(ga)) @ w_proj_att
    y_mem = (_memory_attention(qm, _rmsnorm(mem, mem_norm_g), w_mem_kv, mem_q_norm_g, mem_k_norm_g)
             * jax.nn.silu(gm)) @ w_proj_mem
    g = jax.nn.sigmoid(gmerge + b_merge).reshape(B, S, N_BRANCHES, D_MODEL)
    y = g[:, :, 0] * y_rnn + g[:, :, 1] * y_att + g[:, :, 2] * y_mem
    return x + y @ w_out


def setup_inputs(seed: int = 0) -> dict:
    key = jax.random.key(seed)
    ks = jax.random.split(key, 24)
    f32 = jnp.float32

    def nrm(k, shape, scale):
        return jax.random.normal(k, (DEPTH,) + shape, f32) * scale

    a8 = jax.random.uniform(ks[12], (DEPTH, D_RNN), f32, 0.9, 0.999)
    a = a8 ** (1.0 / LRU_C)
    lru_lambda = jnp.log(a) - jnp.log1p(-a)
    return {
        "x": jax.random.normal(ks[0], (BATCH, SEQ, D_MODEL), f32),
        "mem": jax.random.normal(ks[1], (BATCH, MEM_TOKENS, D_MODEL), f32),
        "norm_g": 1.0 + nrm(ks[2], (D_MODEL,), 0.05),
        "mem_norm_g": 1.0 + nrm(ks[3], (D_MODEL,), 0.05),
        "w_in": nrm(ks[4], (D_MODEL, D_IN), D_MODEL ** -0.5),
        "b_merge": nrm(ks[5], (N_BRANCHES * D_MODEL,), 0.01),
        "conv_w": nrm(ks[6], (CONV_WIDTH, D_RNN), CONV_WIDTH ** -0.5),
        "conv_b": nrm(ks[7], (D_RNN,), 0.01),
        "lru_wa": nrm(ks[8], (RNN_BLOCKS, RNN_BLOCK_DIM, RNN_BLOCK_DIM), RNN_BLOCK_DIM ** -0.5),
        "lru_ba": nrm(ks[9], (D_RNN,), 0.01),
        "lru_wx": nrm(ks[10], (RNN_BLOCKS, RNN_BLOCK_DIM, RNN_BLOCK_DIM), RNN_BLOCK_DIM ** -0.5),
        "lru_bx": nrm(ks[11], (D_RNN,), 0.01),
        "lru_lambda": lru_lambda,
        "q_norm_g": 1.0 + nrm(ks[13], (ATT_HEAD_DIM,), 0.05),
        "k_norm_g": 1.0 + nrm(ks[14], (ATT_HEAD_DIM,), 0.05),
        "rel_bias": nrm(ks[15], (ATT_HEADS, 2 * MAX_REL + 1), 0.1),
        "w_mem_kv": nrm(ks[16], (D_MODEL, 2 * D_MEM), D_MODEL ** -0.5),
        "mem_q_norm_g": 1.0 + nrm(ks[17], (MEM_HEAD_DIM,), 0.05),
        "mem_k_norm_g": 1.0 + nrm(ks[18], (MEM_HEAD_DIM,), 0.05),
        "w_proj_rnn": nrm(ks[19], (D_RNN, D_MODEL), D_RNN ** -0.5),
        "w_proj_att": nrm(ks[20], (D_ATT, D_MODEL), D_ATT ** -0.5),
        "w_proj_mem": nrm(ks[21], (D_MEM, D_MODEL), D_MEM ** -0.5),
        "w_out": nrm(ks[22], (D_MODEL, D_MODEL), D_MODEL ** -0.5),
    }


def reference(x, mem, norm_g, mem_norm_g, w_in, b_merge, conv_w, conv_b, lru_wa, lru_ba,
              lru_wx, lru_bx, lru_lambda, q_norm_g, k_norm_g, rel_bias, w_mem_kv,
              mem_q_norm_g, mem_k_norm_g, w_proj_rnn, w_proj_att, w_proj_mem, w_out):
    for l in range(DEPTH):
        x = _layer(x, mem, norm_g[l], mem_norm_g[l], w_in[l], b_merge[l], conv_w[l], conv_b[l],
                   lru_wa[l], lru_ba[l], lru_wx[l], lru_bx[l], lru_lambda[l], q_norm_g[l],
                   k_norm_g[l], rel_bias[l], w_mem_kv[l], mem_q_norm_g[l], mem_k_norm_g[l],
                   w_proj_rnn[l], w_proj_att[l], w_proj_mem[l], w_out[l])
    return x
```

```python
import functools

import jax
import jax.numpy as jnp
import numpy as np
from jax import lax
from jax.experimental import pallas as pl
from jax.experimental.pallas import tpu as pltpu

D_MODEL = 1024
CHUNK = 64
N_PAST_CHUNKS = 8
PAST = N_PAST_CHUNKS * CHUNK
BAND = PAST + CHUNK
D_RNN = 1024
RNN_BLOCKS = 8
RNN_BLOCK_DIM = D_RNN // RNN_BLOCKS
CONV_WIDTH = 4
LRU_C = 8.0
ATT_HEADS = 8
ATT_HEAD_DIM = 128
D_ATT = ATT_HEADS * ATT_HEAD_DIM
MAX_REL = 256
MEM_HEADS = 4
MEM_HEAD_DIM = 256
D_MEM = MEM_HEADS * MEM_HEAD_DIM
N_BRANCHES = 3
EPS = 1e-6
NEG_INF = -1e30

V7X_SUBLANES = 8
SEQ_BLOCK = 512
Q_BLOCK = 256
Q_KEYS = PAST + Q_BLOCK
COL_CHUNK = 256
VMEM_LIMIT_BYTES = 56 * 1024 * 1024

BF16 = jnp.bfloat16
F32 = jnp.float32


def _dot(a, b):
    return jnp.dot(a, b, preferred_element_type=F32)


def _dot_nt(a, b):
    return lax.dot_general(a, b, (((1,), (1,)), ((), ())), preferred_element_type=F32)


def _rms_rows(xf, g):
    ms = jnp.mean(xf * xf, axis=-1, keepdims=True)
    return xf * lax.rsqrt(ms + EPS) * g


def _const_spec(shape):
    nd = len(shape)
    return pl.BlockSpec(shape, lambda *_: (0,) * nd)


def _token_spec(width):
    return pl.BlockSpec((None, SEQ_BLOCK, width), lambda b, s: (b, s, 0))


def _params():
    return pltpu.CompilerParams(
        dimension_semantics=("arbitrary", "arbitrary"),
        vmem_limit_bytes=VMEM_LIMIT_BYTES,
    )


def _mem_kv_kernel(mem_ref, g_ref, w_ref, kg_ref, k_ref, v_ref):
    mn = _rms_rows(mem_ref[...], g_ref[...]).astype(BF16)
    for h in range(MEM_HEADS):
        c0 = h * MEM_HEAD_DIM
        km = _dot(mn, w_ref[:, c0:c0 + MEM_HEAD_DIM])
        k_ref[:, c0:c0 + MEM_HEAD_DIM] = _rms_rows(km, kg_ref[...]).astype(BF16)
        vm = _dot(mn, w_ref[:, D_MEM + c0:D_MEM + c0 + MEM_HEAD_DIM])
        v_ref[:, c0:c0 + MEM_HEAD_DIM] = vm.astype(BF16)


def _mem_kv(mem, mem_norm_g, w_mem_kv, mem_k_norm_g):
    B, M, _ = mem.shape
    blk = pl.BlockSpec((None, M, D_MODEL), lambda b: (b, 0, 0))
    return pl.pallas_call(
        _mem_kv_kernel,
        grid=(B,),
        in_specs=[blk, _const_spec((1, D_MODEL)), _const_spec((D_MODEL, 2 * D_MEM)),
                  _const_spec((1, MEM_HEAD_DIM))],
        out_specs=[pl.BlockSpec((None, M, D_MEM), lambda b: (b, 0, 0))] * 2,
        out_shape=[jax.ShapeDtypeStruct((B, M, D_MEM), BF16)] * 2,
        compiler_params=pltpu.CompilerParams(dimension_semantics=("arbitrary",),
                                             vmem_limit_bytes=VMEM_LIMIT_BYTES),
        name="mem_kv",
    )(mem, mem_norm_g, w_mem_kv, mem_k_norm_g)


def _rnn_kernel(x_ref, ng_ref, w_ref, cw_ref, cb_ref, wg_ref, bg_ref, lam_ref, o_ref,
                xr_buf, a_buf, b_buf, carry_ref):
    T = SEQ_BLOCK
    pad = V7X_SUBLANES

    @pl.when(pl.program_id(1) == 0)
    def _():
        xr_buf[0:pad, :] = jnp.zeros((pad, D_RNN), F32)
        carry_ref[...] = jnp.zeros((pad, D_RNN), F32)

    h = _rms_rows(x_ref[...], ng_ref[...]).astype(BF16)

    for c in range(D_RNN // COL_CHUNK):
        c0 = c * COL_CHUNK
        xr_buf[pad:pad + T, c0:c0 + COL_CHUNK] = _dot(h, w_ref[:, c0:c0 + COL_CHUNK])

    sp = jax.nn.softplus(-lam_ref[...])
    for n in range(RNN_BLOCKS):
        c0 = n * RNN_BLOCK_DIM
        cs = slice(c0, c0 + RNN_BLOCK_DIM)
        xc = cb_ref[:, cs]
        for j in range(CONV_WIDTH):
            r0 = pad - (CONV_WIDTH - 1) + j
            xc = xc + cw_ref[j:j + 1, cs] * xr_buf[r0:r0 + T, cs]
        gates = _dot(xc.astype(BF16), wg_ref[n]) + bg_ref[n]
        r = jax.nn.sigmoid(gates[:, :RNN_BLOCK_DIM])
        i = jax.nn.sigmoid(gates[:, RNN_BLOCK_DIM:])
        log_a = (-LRU_C * r) * sp[:, cs]
        a = jnp.exp(log_a)
        one_minus_a2 = jnp.tanh(-log_a) * (1.0 + a * a)
        a_buf[:, cs] = a
        b_buf[:, cs] = jnp.sqrt(one_minus_a2) * (i * xc)
    xr_buf[0:pad, :] = xr_buf[T:T + pad, :]

    row = lax.broadcasted_iota(jnp.int32, (pad, D_RNN), 0)

    def tile_step(g, carry):
        r0 = pl.multiple_of(g * pad, pad)
        a = a_buf[pl.ds(r0, pad), :]
        b = b_buf[pl.ds(r0, pad), :]
        for k in (1, 2, 4):
            keep = row >= k
            b = a * jnp.where(keep, pltpu.roll(b, k, 0), 0.0) + b
            a = a * jnp.where(keep, pltpu.roll(a, k, 0), 1.0)
        hh = a * carry + b
        b_buf[pl.ds(r0, pad), :] = hh
        return jnp.broadcast_to(hh[pad - 1:pad, :], (pad, D_RNN))

    carry_ref[...] = lax.fori_loop(0, T // pad, tile_step, carry_ref[...], unroll=2)

    for c in range(D_RNN // COL_CHUNK):
        c0 = c * COL_CHUNK
        gr = _dot(h, w_ref[:, D_RNN + c0:D_RNN + c0 + COL_CHUNK])
        o_ref[:, c0:c0 + COL_CHUNK] = (b_buf[:, c0:c0 + COL_CHUNK] * jax.nn.silu(gr)).astype(BF16)


def _rnn_branch(x, norm_g, w_rnn, conv_w, conv_b, w_gates, b_gates, lru_lambda):
    B, S, _ = x.shape
    return pl.pallas_call(
        _rnn_kernel,
        grid=(B, S // SEQ_BLOCK),
        in_specs=[_token_spec(D_MODEL), _const_spec((1, D_MODEL)), _const_spec((D_MODEL, 2 * D_RNN)),
                  _const_spec((CONV_WIDTH, D_RNN)), _const_spec((1, D_RNN)),
                  _const_spec((RNN_BLOCKS, RNN_BLOCK_DIM, 2 * RNN_BLOCK_DIM)),
                  _const_spec((RNN_BLOCKS, 1, 2 * RNN_BLOCK_DIM)), _const_spec((1, D_RNN))],
        out_specs=_token_spec(D_RNN),
        out_shape=jax.ShapeDtypeStruct((B, S, D_RNN), BF16),
        scratch_shapes=[pltpu.VMEM((SEQ_BLOCK + V7X_SUBLANES, D_RNN), F32),
                        pltpu.VMEM((SEQ_BLOCK, D_RNN), F32),
                        pltpu.VMEM((SEQ_BLOCK, D_RNN), F32),
                        pltpu.VMEM((V7X_SUBLANES, D_RNN), F32)],
        compiler_params=_params(),
        name="rnn_branch",
    )(x, norm_g, w_rnn, conv_w, conv_b, w_gates, b_gates, lru_lambda)


def _att_kernel(x_ref, ng_ref, w_ref, qg_ref, kg_ref, bias_ref, o_ref, q_buf, k_buf, v_buf, g_buf):
    T = SEQ_BLOCK
    first = pl.program_id(1) == 0

    @pl.when(first)
    def _():
        k_buf[0:PAST, :] = jnp.zeros((PAST, D_ATT), BF16)
        v_buf[0:PAST, :] = jnp.zeros((PAST, D_ATT), BF16)

    h = _rms_rows(x_ref[...], ng_ref[...]).astype(BF16)
    for c in range(D_ATT // COL_CHUNK):
        c0 = c * COL_CHUNK
        q = _dot(h, w_ref[:, c0:c0 + COL_CHUNK])
        k = _dot(h, w_ref[:, D_ATT + c0:D_ATT + c0 + COL_CHUNK])
        for hh in range(COL_CHUNK // ATT_HEAD_DIM):
            hs = slice(hh * ATT_HEAD_DIM, (hh + 1) * ATT_HEAD_DIM)
            os_ = slice(c0 + hh * ATT_HEAD_DIM, c0 + (hh + 1) * ATT_HEAD_DIM)
            q_buf[:, os_] = _rms_rows(q[:, hs], qg_ref[...]).astype(BF16)
            k_buf[PAST:PAST + T, os_] = _rms_rows(k[:, hs], kg_ref[...]).astype(BF16)
        v_buf[PAST:PAST + T, c0:c0 + COL_CHUNK] = _dot(
            h, w_ref[:, 2 * D_ATT + c0:2 * D_ATT + c0 + COL_CHUNK]).astype(BF16)
        g_buf[:, c0:c0 + COL_CHUNK] = jax.nn.silu(
            _dot(h, w_ref[:, 3 * D_ATT + c0:3 * D_ATT + c0 + COL_CHUNK]))

    scale = ATT_HEAD_DIM ** -0.5
    col = lax.broadcasted_iota(jnp.int32, (Q_BLOCK, Q_KEYS), 1)
    for qb in range(T // Q_BLOCK):
        q0 = qb * Q_BLOCK
        n_before = jnp.where(first, PAST - q0, 0)
        valid = col >= n_before
        for hh in range(ATT_HEADS):
            hs = slice(hh * ATT_HEAD_DIM, (hh + 1) * ATT_HEAD_DIM)
            s = _dot_nt(q_buf[q0:q0 + Q_BLOCK, hs], k_buf[q0:q0 + Q_KEYS, hs]) * scale + bias_ref[hh]
            s = jnp.where(valid, s, NEG_INF)
            e = jnp.exp(s - jnp.max(s, axis=-1, keepdims=True))
            l = jnp.sum(e, axis=-1, keepdims=True)
            o = _dot(e.astype(BF16), v_buf[q0:q0 + Q_KEYS, hs]) / l
            o_ref[q0:q0 + Q_BLOCK, hs] = (o * g_buf[q0:q0 + Q_BLOCK, hs]).astype(BF16)

    k_buf[0:PAST, :] = k_buf[T:T + PAST, :]
    v_buf[0:PAST, :] = v_buf[T:T + PAST, :]


def _att_branch(x, norm_g, w_att, q_norm_g, k_norm_g, bias):
    B, S, _ = x.shape
    return pl.pallas_call(
        _att_kernel,
        grid=(B, S // SEQ_BLOCK),
        in_specs=[_token_spec(D_MODEL), _const_spec((1, D_MODEL)), _const_spec((D_MODEL, 4 * D_ATT)),
                  _const_spec((1, ATT_HEAD_DIM)), _const_spec((1, ATT_HEAD_DIM)),
                  _const_spec((ATT_HEADS, Q_BLOCK, Q_KEYS))],
        out_specs=_token_spec(D_ATT),
        out_shape=jax.ShapeDtypeStruct((B, S, D_ATT), BF16),
        scratch_shapes=[pltpu.VMEM((SEQ_BLOCK, D_ATT), BF16),
                        pltpu.VMEM((PAST + SEQ_BLOCK, D_ATT), BF16),
                        pltpu.VMEM((PAST + SEQ_BLOCK, D_ATT), BF16),
                        pltpu.VMEM((SEQ_BLOCK, D_ATT), F32)],
        compiler_params=_params(),
        name="att_branch",
    )(x, norm_g, w_att, q_norm_g, k_norm_g, bias)


def _band_bias(rel_bias):
    i = np.arange(Q_BLOCK)[:, None]
    j = np.arange(Q_KEYS)[None, :]
    dist = i + PAST - j
    first_key = (i // CHUNK) * CHUNK
    in_band = (j >= first_key) & (j < first_key + BAND)
    table = rel_bias[:, np.clip(dist, -MAX_REL, MAX_REL) + MAX_REL]
    return jnp.where(jnp.asarray(in_band)[None], table, NEG_INF).astype(F32)


def _mem_kernel(x_ref, ng_ref, w_ref, qg_ref, k_ref, v_ref, o_ref):
    h = _rms_rows(x_ref[...], ng_ref[...]).astype(BF16)
    scale = MEM_HEAD_DIM ** -0.5
    for hh in range(MEM_HEADS):
        hs = slice(hh * MEM_HEAD_DIM, (hh + 1) * MEM_HEAD_DIM)
        q = _rms_rows(_dot(h, w_ref[:, hs]), qg_ref[...]).astype(BF16)
        s = _dot_nt(q, k_ref[:, hs]) * scale
        e = jnp.exp(s - jnp.max(s, axis=-1, keepdims=True))
        l = jnp.sum(e, axis=-1, keepdims=True)
        o = _dot(e.astype(BF16), v_ref[:, hs]) / l
        gm = _dot(h, w_ref[:, D_MEM + hh * MEM_HEAD_DIM:D_MEM + (hh + 1) * MEM_HEAD_DIM])
        o_ref[:, hs] = (o * jax.nn.silu(gm)).astype(BF16)


def _mem_branch(x, norm_g, w_mem, mem_q_norm_g, k_mem, v_mem):
    B, S, _ = x.shape
    M = k_mem.shape[1]
    kv_spec = pl.BlockSpec((None, M, D_MEM), lambda b, s: (b, 0, 0))
    return pl.pallas_call(
        _mem_kernel,
        grid=(B, S // SEQ_BLOCK),
        in_specs=[_token_spec(D_MODEL), _const_spec((1, D_MODEL)), _const_spec((D_MODEL, 2 * D_MEM)),
                  _const_spec((1, MEM_HEAD_DIM)), kv_spec, kv_spec],
        out_specs=_token_spec(D_MEM),
        out_shape=jax.ShapeDtypeStruct((B, S, D_MEM), BF16),
        compiler_params=_params(),
        name="mem_branch",
    )(x, norm_g, w_mem, mem_q_norm_g, k_mem, v_mem)


def _out_kernel(x_ref, ng_ref, wg_ref, bm_ref, ar_ref, aa_ref, am_ref, wr_ref, wa_ref, wm_ref, wo_ref,
                o_ref, y_buf):
    x = x_ref[...]
    h = _rms_rows(x, ng_ref[...]).astype(BF16)
    branches = ((ar_ref, wr_ref), (aa_ref, wa_ref), (am_ref, wm_ref))
    for c in range(D_MODEL // COL_CHUNK):
        cs = slice(c * COL_CHUNK, (c + 1) * COL_CHUNK)
        y = None
        for n, (a_ref, w_ref) in enumerate(branches):
            gs = slice(n * D_MODEL + c * COL_CHUNK, n * D_MODEL + (c + 1) * COL_CHUNK)
            gate = jax.nn.sigmoid(_dot(h, wg_ref[:, gs]) + bm_ref[:, gs])
            term = gate * _dot(a_ref[...], w_ref[:, cs])
            y = term if y is None else y + term
        y_buf[:, cs] = y.astype(BF16)
    for c in range(D_MODEL // COL_CHUNK):
        cs = slice(c * COL_CHUNK, (c + 1) * COL_CHUNK)
        o_ref[:, cs] = x[:, cs] + _dot(y_buf[...], wo_ref[:, cs])


def _merge_out(x, norm_g, w_merge, b_merge, a_rnn, a_att, a_mem, w_proj_rnn, w_proj_att, w_proj_mem, w_out):
    B, S, _ = x.shape
    sq = _const_spec((D_MODEL, D_MODEL))
    return pl.pallas_call(
        _out_kernel,
        grid=(B, S // SEQ_BLOCK),
        in_specs=[_token_spec(D_MODEL), _const_spec((1, D_MODEL)),
                  _const_spec((D_MODEL, N_BRANCHES * D_MODEL)), _const_spec((1, N_BRANCHES * D_MODEL)),
                  _token_spec(D_RNN), _token_spec(D_ATT), _token_spec(D_MEM), sq, sq, sq, sq],
        out_specs=_token_spec(D_MODEL),
        out_shape=jax.ShapeDtypeStruct((B, S, D_MODEL), F32),
        scratch_shapes=[pltpu.VMEM((SEQ_BLOCK, D_MODEL), BF16)],
        compiler_params=_params(),
        name="merge_out",
    )(x, norm_g, w_merge, b_merge, a_rnn, a_att, a_mem, w_proj_rnn, w_proj_att, w_proj_mem, w_out)


@jax.jit
def kernel(x, mem, norm_g, mem_norm_g, w_in, b_merge, conv_w, conv_b, lru_wa, lru_ba, lru_wx, lru_bx, lru_lambda, q_norm_g, k_norm_g, rel_bias, w_mem_kv, mem_q_norm_g, mem_k_norm_g, w_proj_rnn, w_proj_att, w_proj_mem, w_out):
    B, S, D = x.shape
    assert D == D_MODEL and S % SEQ_BLOCK == 0 and w_in.shape[0] == 1

    def row(p):
        return p[0].reshape(1, -1)

    w = w_in[0].astype(BF16)
    o_att = 2 * D_RNN
    o_mem = o_att + 4 * D_ATT
    o_mrg = o_mem + 2 * D_MEM
    w_rnn, w_att, w_mem, w_mrg = w[:, :o_att], w[:, o_att:o_mem], w[:, o_mem:o_mrg], w[:, o_mrg:]
    w_gates = jnp.concatenate([lru_wa[0], lru_wx[0]], axis=-1).astype(BF16)
    b_gates = jnp.concatenate([lru_ba[0].reshape(RNN_BLOCKS, 1, RNN_BLOCK_DIM),
                               lru_bx[0].reshape(RNN_BLOCKS, 1, RNN_BLOCK_DIM)], axis=-1)
    ng = row(norm_g)

    k_mem, v_mem = _mem_kv(mem, row(mem_norm_g), w_mem_kv[0].astype(BF16), row(mem_k_norm_g))
    a_rnn = _rnn_branch(x, ng, w_rnn, conv_w[0], row(conv_b), w_gates, b_gates, row(lru_lambda))
    a_att = _att_branch(x, ng, w_att, row(q_norm_g), row(k_norm_g), _band_bias(rel_bias[0]))
    a_mem = _mem_branch(x, ng, w_mem, row(mem_q_norm_g), k_mem, v_mem)
    return _merge_out(x, ng, w_mrg, row(b_merge), a_rnn, a_att, a_mem,
                      w_proj_rnn[0].astype(BF16), w_proj_att[0].astype(BF16),
                      w_proj_mem[0].astype(BF16), w_out[0].astype(BF16))
```

```python
import jax
import jax.numpy as jnp
import numpy as np
from jax import lax
from jax.experimental import pallas as pl
from jax.experimental.pallas import tpu as pltpu

D_MODEL = 1024
CHUNK = 64
N_PAST_CHUNKS = 8
PAST = N_PAST_CHUNKS * CHUNK
BAND = PAST + CHUNK
D_RNN = 1024
RNN_BLOCKS = 8
RNN_BLOCK_DIM = D_RNN // RNN_BLOCKS
CONV_WIDTH = 4
LRU_C = 8.0
ATT_HEADS = 8
ATT_HEAD_DIM = 128
D_ATT = ATT_HEADS * ATT_HEAD_DIM
MAX_REL = 256
MEM_HEADS = 4
MEM_HEAD_DIM = 256
D_MEM = MEM_HEADS * MEM_HEAD_DIM
N_BRANCHES = 3
EPS = 1e-6
NEG_INF = -1e30

V7X_SUBLANES = 8
V7X_MXU_DIM = 256
SEQ_BLOCK = 512
ATT_SEQ_BLOCK = 1024
MEM_SEQ_BLOCK = 1024
Q_BLOCK = 256
Q_KEYS = PAST + Q_BLOCK
COL_CHUNK = V7X_MXU_DIM
VMEM_LIMIT_BYTES = 56 * 1024 * 1024
RNN_BATCH = V7X_SUBLANES
RNN_STEPS = 32
RNN_ROWS = RNN_BATCH * RNN_STEPS
RNN_GROUPS = 2
RNN_TIME_SLICES = 2
CONV_HIST = (CONV_WIDTH - 1) * RNN_BATCH
BIAS_TABLE = 2 * PAST
V_SLOT = 2 * ATT_HEAD_DIM
LOG2E = float(np.log2(np.e))

BF16 = jnp.bfloat16
F32 = jnp.float32


def _dot(a, b):
    return jnp.dot(a, b, preferred_element_type=F32)


def _dot_nt(a, b):
    return lax.dot_general(a, b, (((1,), (1,)), ((), ())), preferred_element_type=F32)


def _rms_rows(xf, g):
    ms = jnp.mean(xf * xf, axis=-1, keepdims=True)
    return xf * lax.rsqrt(ms + EPS) * g


def _const_spec(shape):
    nd = len(shape)
    return pl.BlockSpec(shape, lambda *_: (0,) * nd, pipeline_mode=pl.Buffered(1))


def _token_spec(width, rows=SEQ_BLOCK):
    return pl.BlockSpec((None, rows, width), lambda b, s: (b, s, 0))


def _params():
    return pltpu.CompilerParams(
        dimension_semantics=("arbitrary", "arbitrary"),
        vmem_limit_bytes=VMEM_LIMIT_BYTES,
    )


def _mem_kv_kernel(mem_ref, g_ref, w_ref, kg_ref, k_ref, v_ref):
    mn = _rms_rows(mem_ref[...], g_ref[...]).astype(BF16)
    for h in range(MEM_HEADS):
        c0 = h * MEM_HEAD_DIM
        km = _dot(mn, w_ref[:, c0:c0 + MEM_HEAD_DIM])
        k_ref[:, c0:c0 + MEM_HEAD_DIM] = _rms_rows(km, kg_ref[...]).astype(BF16)
        vm = _dot(mn, w_ref[:, D_MEM + c0:D_MEM + c0 + MEM_HEAD_DIM])
        v_ref[:, c0:c0 + MEM_HEAD_DIM] = vm.astype(BF16)


def _mem_kv(mem, mem_norm_g, w_mem_kv, mem_k_norm_g):
    B, M, _ = mem.shape
    blk = pl.BlockSpec((None, M, D_MODEL), lambda b: (b, 0, 0))
    return pl.pallas_call(
        _mem_kv_kernel,
        grid=(B,),
        in_specs=[blk, _const_spec((1, D_MODEL)), _const_spec((D_MODEL, 2 * D_MEM)),
                  _const_spec((1, MEM_HEAD_DIM))],
        out_specs=[pl.BlockSpec((None, M, D_MEM), lambda b: (b, 0, 0))] * 2,
        out_shape=[jax.ShapeDtypeStruct((B, M, D_MEM), BF16)] * 2,
        compiler_params=pltpu.CompilerParams(dimension_semantics=("arbitrary",),
                                             vmem_limit_bytes=VMEM_LIMIT_BYTES),
        name="mem_kv",
    )(mem, mem_norm_g, w_mem_kv, mem_k_norm_g)


def _rnn_kernel(x_ref, ng_ref, p_ref, pt_ref, w_ref, cw_ref, cb_ref, wg_ref, bg_ref, lam_ref, o_ref,
                xr_buf, a_buf, b_buf, sg_buf, gate_buf, carry_ref):
    R = RNN_ROWS

    @pl.when(pl.program_id(1) == 0)
    def _():
        xr_buf[:, 0:CONV_HIST, :] = jnp.zeros((RNN_GROUPS, CONV_HIST, D_RNN), F32)
        carry_ref[...] = jnp.zeros(carry_ref.shape, F32)

    neg_log_a = LRU_C * jax.nn.softplus(-lam_ref[...])
    a_exp = -LOG2E * neg_log_a

    def sub_block(k):
        g, th = k % RNN_GROUPS, k // RNN_GROUPS
        return g, slice(g * RNN_BATCH, (g + 1) * RNN_BATCH), slice(th * RNN_STEPS, (th + 1) * RNN_STEPS)

    def project(k):
        g, bs, ts = sub_block(k)
        xr_g, sg_g = xr_buf.at[g], sg_buf.at[g]
        h = _rms_rows(x_ref[bs, ts, :].reshape(R, D_MODEL), ng_ref[...]).astype(BF16)
        hp = _dot(p_ref[...], h).astype(BF16)
        for c in range(D_RNN // COL_CHUNK):
            c0 = c * COL_CHUNK
            xr_g[CONV_HIST:CONV_HIST + R, c0:c0 + COL_CHUNK] = _dot(hp, w_ref[:, c0:c0 + COL_CHUNK])
            sg_g[:, c0:c0 + COL_CHUNK] = jax.nn.silu(_dot(hp, w_ref[:, D_RNN + c0:D_RNN + c0 + COL_CHUNK]))

    def conv_gates(k):
        g, _, _ = sub_block(k)
        xr_g, xc_g, gt_g = xr_buf.at[g], a_buf.at[g], gate_buf.at[g]
        for n in range(RNN_BLOCKS):
            cs = slice(n * RNN_BLOCK_DIM, (n + 1) * RNN_BLOCK_DIM)
            xc = cb_ref[:, cs]
            for j in range(CONV_WIDTH):
                xc = xc + cw_ref[j:j + 1, cs] * xr_g[j * RNN_BATCH:j * RNN_BATCH + R, cs]
            xc_g[:, cs] = xc
            gt_g[:, 2 * n * RNN_BLOCK_DIM:2 * (n + 1) * RNN_BLOCK_DIM] = (
                _dot(xc.astype(BF16), wg_ref[n]) + bg_ref[n])
        xr_g[0:CONV_HIST, :] = xr_g[R:R + CONV_HIST, :]

    def recur(k):
        g, bs, ts = sub_block(k)
        a_g, b_g, sg_g, gt_g = a_buf.at[g], b_buf.at[g], sg_buf.at[g], gate_buf.at[g]
        for n in range(RNN_BLOCKS):
            cs = slice(n * RNN_BLOCK_DIM, (n + 1) * RNN_BLOCK_DIM)
            xc = a_g[:, cs]
            r = jax.nn.sigmoid(gt_g[:, 2 * n * RNN_BLOCK_DIM:(2 * n + 1) * RNN_BLOCK_DIM])
            i = jax.nn.sigmoid(gt_g[:, (2 * n + 1) * RNN_BLOCK_DIM:(2 * n + 2) * RNN_BLOCK_DIM])
            a = jnp.exp2(r * a_exp[:, cs])
            one_minus_a2 = jnp.tanh(r * neg_log_a[:, cs]) * (1.0 + a * a)
            a_g[:, cs] = a
            b_g[:, cs] = jnp.sqrt(one_minus_a2) * (i * xc)

        hc = carry_ref[g]
        for t in range(RNN_STEPS):
            rows = slice(t * RNN_BATCH, (t + 1) * RNN_BATCH)
            hc = a_g[rows, :] * hc + b_g[rows, :]
            b_g[rows, :] = hc
        carry_ref[g] = hc

        for c in range(D_RNN // COL_CHUNK):
            c0 = c * COL_CHUNK
            y = (b_g[:, c0:c0 + COL_CHUNK] * sg_g[:, c0:c0 + COL_CHUNK]).astype(BF16)
            y = _dot(pt_ref[...], y)
            o_ref[bs, ts, c0:c0 + COL_CHUNK] = y.reshape(RNN_BATCH, RNN_STEPS, COL_CHUNK).astype(BF16)

    n_sub = RNN_GROUPS * RNN_TIME_SLICES
    project(0)
    conv_gates(0)
    for k in range(n_sub):
        if k + 1 < n_sub:
            project(k + 1)
        recur(k)
        if k + 1 < n_sub:
            conv_gates(k + 1)


def _row_shuffle():
    out_row = np.arange(RNN_ROWS)
    t, b = out_row // RNN_BATCH, out_row % RNN_BATCH
    p = np.zeros((RNN_ROWS, RNN_ROWS), np.float32)
    p[out_row, b * RNN_STEPS + t] = 1.0
    return p


def _rnn_branch(x, norm_g, w_rnn, conv_w, conv_b, w_gates, b_gates, lru_lambda):
    B, S, _ = x.shape
    p = _row_shuffle()
    steps = RNN_TIME_SLICES * RNN_STEPS
    blk = lambda width: pl.BlockSpec((RNN_GROUPS * RNN_BATCH, steps, width), lambda b, s: (b, s, 0))
    return pl.pallas_call(
        _rnn_kernel,
        grid=(B // (RNN_GROUPS * RNN_BATCH), S // steps),
        in_specs=[blk(D_MODEL), _const_spec((1, D_MODEL)),
                  _const_spec((RNN_ROWS, RNN_ROWS)), _const_spec((RNN_ROWS, RNN_ROWS)),
                  _const_spec((D_MODEL, 2 * D_RNN)),
                  _const_spec((CONV_WIDTH, D_RNN)), _const_spec((1, D_RNN)),
                  _const_spec((RNN_BLOCKS, RNN_BLOCK_DIM, 2 * RNN_BLOCK_DIM)),
                  _const_spec((RNN_BLOCKS, 1, 2 * RNN_BLOCK_DIM)), _const_spec((1, D_RNN))],
        out_specs=blk(D_RNN),
        out_shape=jax.ShapeDtypeStruct((B, S, D_RNN), BF16),
        scratch_shapes=[pltpu.VMEM((RNN_GROUPS, CONV_HIST + RNN_ROWS, D_RNN), F32),
                        pltpu.VMEM((RNN_GROUPS, RNN_ROWS, D_RNN), F32),
                        pltpu.VMEM((RNN_GROUPS, RNN_ROWS, D_RNN), F32),
                        pltpu.VMEM((RNN_GROUPS, RNN_ROWS, D_RNN), F32),
                        pltpu.VMEM((RNN_GROUPS, RNN_ROWS, 2 * D_RNN), F32),
                        pltpu.VMEM((RNN_GROUPS, RNN_BATCH, D_RNN), F32)],
        compiler_params=_params(),
        name="rnn_branch",
    )(x, norm_g, jnp.asarray(p, BF16), jnp.asarray(p.T, BF16), w_rnn, conv_w, conv_b, w_gates, b_gates,
      lru_lambda)


def _att_kernel(x_ref, ng_ref, w_ref, qg_ref, kg_ref, tbl_ref, o_ref,
                q_buf, k_buf, v_buf, g_buf, bias_buf, h_buf):
    T = ATT_SEQ_BLOCK
    first = pl.program_id(1) == 0

    @pl.when(jnp.logical_and(first, pl.program_id(0) == 0))
    def _():
        i = lax.broadcasted_iota(jnp.int32, (Q_BLOCK, Q_KEYS), 0)
        j = lax.broadcasted_iota(jnp.int32, (Q_BLOCK, Q_KEYS), 1)
        off = j - (i // CHUNK) * CHUNK
        in_band = jnp.logical_and(off >= 0, off < BAND)
        for hh in range(ATT_HEADS):
            row = jnp.broadcast_to(tbl_ref[hh:hh + 1, :], (Q_BLOCK, BIAS_TABLE))
            win = pltpu.roll(row, BIAS_TABLE - (Q_BLOCK - 1), 1, stride=1, stride_axis=0)
            bias_buf[hh] = jnp.where(in_band, win[:, :Q_KEYS] * LOG2E, NEG_INF)
        v_buf[...] = jnp.ones(v_buf.shape, BF16)

    @pl.when(first)
    def _():
        k_buf[0:PAST, :] = jnp.zeros((PAST, D_ATT), BF16)
        for hh in range(ATT_HEADS):
            v_buf[0:PAST, hh * V_SLOT:hh * V_SLOT + ATT_HEAD_DIM] = jnp.zeros((PAST, ATT_HEAD_DIM), BF16)

    q_scale = ATT_HEAD_DIM ** -0.5 * LOG2E
    heads_per_chunk = COL_CHUNK // ATT_HEAD_DIM

    n_pieces = 2 * (D_ATT // COL_CHUNK)

    def project_piece(qb, piece):
        q0 = qb * Q_BLOCK
        rows = slice(q0, q0 + Q_BLOCK)
        new = slice(PAST + q0, PAST + q0 + Q_BLOCK)
        c, second_half = divmod(piece, 2)
        c0 = c * COL_CHUNK
        if piece == 0:
            h_buf[...] = _rms_rows(x_ref[rows, :], ng_ref[...]).astype(BF16)
        h = h_buf[...]
        if not second_half:
            q = _dot(h, w_ref[:, c0:c0 + COL_CHUNK])
            k = _dot(h, w_ref[:, D_ATT + c0:D_ATT + c0 + COL_CHUNK])
            for hh in range(heads_per_chunk):
                hs = slice(hh * ATT_HEAD_DIM, (hh + 1) * ATT_HEAD_DIM)
                os_ = slice(c0 + hh * ATT_HEAD_DIM, c0 + (hh + 1) * ATT_HEAD_DIM)
                q_buf[rows, os_] = (_rms_rows(q[:, hs], qg_ref[...]) * q_scale).astype(BF16)
                k_buf[new, os_] = _rms_rows(k[:, hs], kg_ref[...]).astype(BF16)
        else:
            v = _dot(h, w_ref[:, 2 * D_ATT + c0:2 * D_ATT + c0 + COL_CHUNK])
            for hh in range(heads_per_chunk):
                hs = slice(hh * ATT_HEAD_DIM, (hh + 1) * ATT_HEAD_DIM)
                head = c * heads_per_chunk + hh
                v_buf[new, head * V_SLOT:head * V_SLOT + ATT_HEAD_DIM] = v[:, hs].astype(BF16)
            g_buf[rows, c0:c0 + COL_CHUNK] = jax.nn.silu(
                _dot(h, w_ref[:, 3 * D_ATT + c0:3 * D_ATT + c0 + COL_CHUNK]))

    col = lax.broadcasted_iota(jnp.int32, (Q_BLOCK, Q_KEYS), 1)

    def scores(qb, hh):
        q0 = qb * Q_BLOCK
        hs = slice(hh * ATT_HEAD_DIM, (hh + 1) * ATT_HEAD_DIM)
        s = _dot_nt(q_buf[q0:q0 + Q_BLOCK, hs], k_buf[q0:q0 + Q_KEYS, hs]) + bias_buf[hh]
        if q0 < PAST:
            s = jnp.where(col >= jnp.where(first, PAST - q0, 0), s, NEG_INF)
        return s

    def finish(qb, hh, s):
        q0 = qb * Q_BLOCK
        hs = slice(hh * ATT_HEAD_DIM, (hh + 1) * ATT_HEAD_DIM)
        e = jnp.exp2(s - jnp.max(s, axis=-1, keepdims=True))
        oe = _dot(e.astype(BF16), v_buf[q0:q0 + Q_KEYS, hh * V_SLOT:(hh + 1) * V_SLOT])
        o = oe[:, :ATT_HEAD_DIM] / oe[:, ATT_HEAD_DIM:]
        o_ref[q0:q0 + Q_BLOCK, hs] = (o * g_buf[q0:q0 + Q_BLOCK, hs]).astype(BF16)

    assert n_pieces == ATT_HEADS
    order = [(qb, hh) for qb in range(T // Q_BLOCK) for hh in range(ATT_HEADS)]
    for piece in range(n_pieces):
        project_piece(0, piece)
    s = scores(*order[0])
    for n, (qb, hh) in enumerate(order):
        if (qb + 1) * Q_BLOCK < T:
            project_piece(qb + 1, hh)
        s_next = scores(*order[n + 1]) if n + 1 < len(order) else None
        finish(qb, hh, s)
        s = s_next

    k_buf[0:PAST, :] = k_buf[T:T + PAST, :]
    for hh in range(ATT_HEADS):
        vs = slice(hh * V_SLOT, hh * V_SLOT + ATT_HEAD_DIM)
        v_buf[0:PAST, vs] = v_buf[T:T + PAST, vs]


def _att_branch(x, norm_g, w_att, q_norm_g, k_norm_g, bias_table):
    B, S, _ = x.shape
    return pl.pallas_call(
        _att_kernel,
        grid=(B, S // ATT_SEQ_BLOCK),
        in_specs=[_token_spec(D_MODEL, ATT_SEQ_BLOCK), _const_spec((1, D_MODEL)),
                  _const_spec((D_MODEL, 4 * D_ATT)),
                  _const_spec((1, ATT_HEAD_DIM)), _const_spec((1, ATT_HEAD_DIM)),
                  _const_spec((ATT_HEADS, BIAS_TABLE))],
        out_specs=_token_spec(D_ATT, ATT_SEQ_BLOCK),
        out_shape=jax.ShapeDtypeStruct((B, S, D_ATT), BF16),
        scratch_shapes=[pltpu.VMEM((ATT_SEQ_BLOCK, D_ATT), BF16),
                        pltpu.VMEM((PAST + ATT_SEQ_BLOCK, D_ATT), BF16),
                        pltpu.VMEM((PAST + ATT_SEQ_BLOCK, ATT_HEADS * V_SLOT), BF16),
                        pltpu.VMEM((ATT_SEQ_BLOCK, D_ATT), F32),
                        pltpu.VMEM((ATT_HEADS, Q_BLOCK, Q_KEYS), F32),
                        pltpu.VMEM((Q_BLOCK, D_MODEL), BF16)],
        compiler_params=_params(),
        name="att_branch",
    )(x, norm_g, w_att, q_norm_g, k_norm_g, bias_table)


def _bias_table(rel_bias):
    assert Q_KEYS - 1 - MAX_REL == BIAS_TABLE - (2 * MAX_REL + 1)
    return jnp.pad(rel_bias[:, ::-1], ((0, 0), (Q_KEYS - 1 - MAX_REL, 0)), mode="edge")


def _mem_kernel(x_ref, ng_ref, w_ref, qg_ref, k_ref, v_ref, o_ref):
    h = _rms_rows(x_ref[...], ng_ref[...]).astype(BF16)
    q_scale = MEM_HEAD_DIM ** -0.5 * LOG2E
    def head_slice(hh):
        return slice(hh * MEM_HEAD_DIM, (hh + 1) * MEM_HEAD_DIM)

    def q_head(hh):
        return (_rms_rows(_dot(h, w_ref[:, head_slice(hh)]), qg_ref[...]) * q_scale).astype(BF16)

    q = q_head(0)
    for hh in range(MEM_HEADS):
        hs = head_slice(hh)
        sg = jax.nn.silu(_dot(h, w_ref[:, D_MEM + hh * MEM_HEAD_DIM:D_MEM + (hh + 1) * MEM_HEAD_DIM]))
        q_next = q_head(hh + 1) if hh + 1 < MEM_HEADS else None
        s = _dot_nt(q, k_ref[:, hs])
        e = jnp.exp2(s - jnp.max(s, axis=-1, keepdims=True))
        l = jnp.sum(e, axis=-1, keepdims=True)
        o = _dot(e.astype(BF16), v_ref[:, hs]) / l
        o_ref[:, hs] = (o * sg).astype(BF16)
        q = q_next


def _mem_branch(x, norm_g, w_mem, mem_q_norm_g, k_mem, v_mem):
    B, S, _ = x.shape
    M = k_mem.shape[1]
    kv_spec = pl.BlockSpec((None, M, D_MEM), lambda b, s: (b, 0, 0))
    return pl.pallas_call(
        _mem_kernel,
        grid=(B, S // MEM_SEQ_BLOCK),
        in_specs=[_token_spec(D_MODEL, MEM_SEQ_BLOCK), _const_spec((1, D_MODEL)),
                  _const_spec((D_MODEL, 2 * D_MEM)), _const_spec((1, MEM_HEAD_DIM)), kv_spec, kv_spec],
        out_specs=_token_spec(D_MEM, MEM_SEQ_BLOCK),
        out_shape=jax.ShapeDtypeStruct((B, S, D_MEM), BF16),
        compiler_params=_params(),
        name="mem_branch",
    )(x, norm_g, w_mem, mem_q_norm_g, k_mem, v_mem)


def _out_kernel(x_ref, ng_ref, wg_ref, bm_ref, ar_ref, aa_ref, am_ref, wr_ref, wa_ref, wm_ref, wo_ref,
                o_ref, y_buf):
    x = x_ref[...]
    h = _rms_rows(x, ng_ref[...]).astype(BF16)
    branches = ((ar_ref, wr_ref), (aa_ref, wa_ref), (am_ref, wm_ref))
    for c in range(D_MODEL // COL_CHUNK):
        cs = slice(c * COL_CHUNK, (c + 1) * COL_CHUNK)
        y = None
        for n, (a_ref, w_ref) in enumerate(branches):
            gs = slice(n * D_MODEL + c * COL_CHUNK, n * D_MODEL + (c + 1) * COL_CHUNK)
            gate = jax.nn.sigmoid(_dot(h, wg_ref[:, gs]) + bm_ref[:, gs])
            term = gate * _dot(a_ref[...], w_ref[:, cs])
            y = term if y is None else y + term
        y_buf[:, cs] = y.astype(BF16)
    for c in range(D_MODEL // COL_CHUNK):
        cs = slice(c * COL_CHUNK, (c + 1) * COL_CHUNK)
        o_ref[:, cs] = x[:, cs] + _dot(y_buf[...], wo_ref[:, cs])


def _merge_out(x, norm_g, w_merge, b_merge, a_rnn, a_att, a_mem, w_proj_rnn, w_proj_att, w_proj_mem, w_out):
    B, S, _ = x.shape
    sq = _const_spec((D_MODEL, D_MODEL))
    return pl.pallas_call(
        _out_kernel,
        grid=(B, S // SEQ_BLOCK),
        in_specs=[_token_spec(D_MODEL), _const_spec((1, D_MODEL)),
                  _const_spec((D_MODEL, N_BRANCHES * D_MODEL)), _const_spec((1, N_BRANCHES * D_MODEL)),
                  _token_spec(D_RNN), _token_spec(D_ATT), _token_spec(D_MEM), sq, sq, sq, sq],
        out_specs=_token_spec(D_MODEL),
        out_shape=jax.ShapeDtypeStruct((B, S, D_MODEL), F32),
        scratch_shapes=[pltpu.VMEM((SEQ_BLOCK, D_MODEL), BF16)],
        compiler_params=_params(),
        name="merge_out",
    )(x, norm_g, w_merge, b_merge, a_rnn, a_att, a_mem, w_proj_rnn, w_proj_att, w_proj_mem, w_out)


@jax.jit
def kernel(x, mem, norm_g, mem_norm_g, w_in, b_merge, conv_w, conv_b, lru_wa, lru_ba, lru_wx, lru_bx, lru_lambda, q_norm_g, k_norm_g, rel_bias, w_mem_kv, mem_q_norm_g, mem_k_norm_g, w_proj_rnn, w_proj_att, w_proj_mem, w_out):
    B, S, D = x.shape
    assert D == D_MODEL and S % SEQ_BLOCK == 0 and B % (RNN_GROUPS * RNN_BATCH) == 0 and w_in.shape[0] == 1

    def row(p):
        return p[0].reshape(1, -1)

    w = w_in[0].astype(BF16)
    o_att = 2 * D_RNN
    o_mem = o_att + 4 * D_ATT
    o_mrg = o_mem + 2 * D_MEM
    w_rnn, w_att, w_mem, w_mrg = w[:, :o_att], w[:, o_att:o_mem], w[:, o_mem:o_mrg], w[:, o_mrg:]
    w_gates = jnp.concatenate([lru_wa[0], lru_wx[0]], axis=-1).astype(BF16)
    b_gates = jnp.concatenate([lru_ba[0].reshape(RNN_BLOCKS, 1, RNN_BLOCK_DIM),
                               lru_bx[0].reshape(RNN_BLOCKS, 1, RNN_BLOCK_DIM)], axis=-1)
    ng = row(norm_g)

    k_mem, v_mem = _mem_kv(mem, row(mem_norm_g), w_mem_kv[0].astype(BF16), row(mem_k_norm_g))
    a_rnn = _rnn_branch(x, ng, w_rnn, conv_w[0], row(conv_b), w_gates, b_gates, row(lru_lambda))
    a_att = _att_branch(x, ng, w_att, row(q_norm_g), row(k_norm_g), _bias_table(rel_bias[0]))
    a_mem = _mem_branch(x, ng, w_mem, row(mem_q_norm_g), k_mem, v_mem)
    return _merge_out(x, ng, w_mrg, row(b_merge), a_rnn, a_att, a_mem,
                      w_proj_rnn[0].astype(BF16), w_proj_att[0].astype(BF16),
                      w_proj_mem[0].astype(BF16), w_out[0].astype(BF16))
```

```python
import jax
import jax.numpy as jnp
import numpy as np
from jax import lax
from jax.experimental import pallas as pl
from jax.experimental.pallas import tpu as pltpu

D_MODEL = 1024
CHUNK = 64
N_PAST_CHUNKS = 8
PAST = N_PAST_CHUNKS * CHUNK
BAND = PAST + CHUNK
D_RNN = 1024
RNN_BLOCKS = 8
RNN_BLOCK_DIM = D_RNN // RNN_BLOCKS
CONV_WIDTH = 4
LRU_C = 8.0
ATT_HEADS = 8
ATT_HEAD_DIM = 128
D_ATT = ATT_HEADS * ATT_HEAD_DIM
MAX_REL = 256
MEM_HEADS = 4
MEM_HEAD_DIM = 256
D_MEM = MEM_HEADS * MEM_HEAD_DIM
N_BRANCHES = 3
EPS = 1e-6
NEG_INF = -1e30

V7X_SUBLANES = 8
V7X_MXU_DIM = 256
ATT_SEQ_BLOCK = 1024
MEM_SEQ_BLOCK = 1024
Q_BLOCK = 256
Q_KEYS = PAST + Q_BLOCK
COL_CHUNK = V7X_MXU_DIM
VMEM_LIMIT_BYTES = 56 * 1024 * 1024
RNN_BATCH = V7X_SUBLANES
RNN_STEPS = 32
RNN_ROWS = RNN_BATCH * RNN_STEPS
RNN_GROUPS = 2
CONV_HIST = (CONV_WIDTH - 1) * RNN_BATCH
BIAS_TABLE = 2 * PAST
V_SLOT = 2 * ATT_HEAD_DIM
LOG2E = float(np.log2(np.e))

BF16 = jnp.bfloat16
F32 = jnp.float32


def _dot(a, b):
    return jnp.dot(a, b, preferred_element_type=F32)


def _dot_nt(a, b):
    return lax.dot_general(a, b, (((1,), (1,)), ((), ())), preferred_element_type=F32)


def _rms_rows(xf, g):
    ms = jnp.mean(xf * xf, axis=-1, keepdims=True)
    return xf * lax.rsqrt(ms + EPS) * g


def _const_spec(shape):
    nd = len(shape)
    return pl.BlockSpec(shape, lambda *_: (0,) * nd, pipeline_mode=pl.Buffered(1))


def _token_spec(width, rows):
    return pl.BlockSpec((None, rows, width), lambda b, s: (b, s, 0))


def _params():
    return pltpu.CompilerParams(
        dimension_semantics=("arbitrary", "arbitrary"),
        vmem_limit_bytes=VMEM_LIMIT_BYTES,
    )


def _mem_kv_kernel(mem_ref, g_ref, w_ref, kg_ref, k_ref, v_ref):
    mn = _rms_rows(mem_ref[...], g_ref[...]).astype(BF16)
    for h in range(MEM_HEADS):
        c0 = h * MEM_HEAD_DIM
        km = _dot(mn, w_ref[:, c0:c0 + MEM_HEAD_DIM])
        k_ref[:, c0:c0 + MEM_HEAD_DIM] = _rms_rows(km, kg_ref[...]).astype(BF16)
        vm = _dot(mn, w_ref[:, D_MEM + c0:D_MEM + c0 + MEM_HEAD_DIM])
        v_ref[:, c0:c0 + MEM_HEAD_DIM] = vm.astype(BF16)


def _mem_kv(mem, mem_norm_g, w_mem_kv, mem_k_norm_g):
    B, M, _ = mem.shape
    blk = pl.BlockSpec((None, M, D_MODEL), lambda b: (b, 0, 0))
    return pl.pallas_call(
        _mem_kv_kernel,
        grid=(B,),
        in_specs=[blk, _const_spec((1, D_MODEL)), _const_spec((D_MODEL, 2 * D_MEM)),
                  _const_spec((1, MEM_HEAD_DIM))],
        out_specs=[pl.BlockSpec((None, M, D_MEM), lambda b: (b, 0, 0))] * 2,
        out_shape=[jax.ShapeDtypeStruct((B, M, D_MEM), BF16)] * 2,
        compiler_params=pltpu.CompilerParams(dimension_semantics=("arbitrary",),
                                             vmem_limit_bytes=VMEM_LIMIT_BYTES),
        name="mem_kv",
    )(mem, mem_norm_g, w_mem_kv, mem_k_norm_g)


def _att_kernel(x_ref, ng_ref, w_ref, qg_ref, kg_ref, tbl_ref, o_ref,
                q_buf, k_buf, v_buf, g_buf, bias_buf, h_buf):
    T = ATT_SEQ_BLOCK
    first = pl.program_id(1) == 0

    @pl.when(jnp.logical_and(first, pl.program_id(0) == 0))
    def _():
        i = lax.broadcasted_iota(jnp.int32, (Q_BLOCK, Q_KEYS), 0)
        j = lax.broadcasted_iota(jnp.int32, (Q_BLOCK, Q_KEYS), 1)
        off = j - (i // CHUNK) * CHUNK
        in_band = jnp.logical_and(off >= 0, off < BAND)
        for hh in range(ATT_HEADS):
            row = jnp.broadcast_to(tbl_ref[hh:hh + 1, :], (Q_BLOCK, BIAS_TABLE))
            win = pltpu.roll(row, BIAS_TABLE - (Q_BLOCK - 1), 1, stride=1, stride_axis=0)
            bias_buf[hh] = jnp.where(in_band, win[:, :Q_KEYS] * LOG2E, NEG_INF)
        v_buf[...] = jnp.ones(v_buf.shape, BF16)

    @pl.when(first)
    def _():
        k_buf[0:PAST, :] = jnp.zeros((PAST, D_ATT), BF16)
        for hh in range(ATT_HEADS):
            v_buf[0:PAST, hh * V_SLOT:hh * V_SLOT + ATT_HEAD_DIM] = jnp.zeros((PAST, ATT_HEAD_DIM), BF16)

    q_scale = ATT_HEAD_DIM ** -0.5 * LOG2E
    heads_per_chunk = COL_CHUNK // ATT_HEAD_DIM
    n_pieces = 2 * (D_ATT // COL_CHUNK)

    def project_piece(qb, piece):
        q0 = qb * Q_BLOCK
        rows = slice(q0, q0 + Q_BLOCK)
        new = slice(PAST + q0, PAST + q0 + Q_BLOCK)
        c, second_half = divmod(piece, 2)
        c0 = c * COL_CHUNK
        if piece == 0:
            h_buf[...] = _rms_rows(x_ref[rows, :], ng_ref[...]).astype(BF16)
        h = h_buf[...]
        if not second_half:
            q = _dot(h, w_ref[:, c0:c0 + COL_CHUNK])
            k = _dot(h, w_ref[:, D_ATT + c0:D_ATT + c0 + COL_CHUNK])
            for hh in range(heads_per_chunk):
                hs = slice(hh * ATT_HEAD_DIM, (hh + 1) * ATT_HEAD_DIM)
                os_ = slice(c0 + hh * ATT_HEAD_DIM, c0 + (hh + 1) * ATT_HEAD_DIM)
                q_buf[rows, os_] = (_rms_rows(q[:, hs], qg_ref[...]) * q_scale).astype(BF16)
                k_buf[new, os_] = _rms_rows(k[:, hs], kg_ref[...]).astype(BF16)
        else:
            v = _dot(h, w_ref[:, 2 * D_ATT + c0:2 * D_ATT + c0 + COL_CHUNK])
            for hh in range(heads_per_chunk):
                hs = slice(hh * ATT_HEAD_DIM, (hh + 1) * ATT_HEAD_DIM)
                head = c * heads_per_chunk + hh
                v_buf[new, head * V_SLOT:head * V_SLOT + ATT_HEAD_DIM] = v[:, hs].astype(BF16)
            g_buf[rows, c0:c0 + COL_CHUNK] = jax.nn.silu(
                _dot(h, w_ref[:, 3 * D_ATT + c0:3 * D_ATT + c0 + COL_CHUNK]))

    col = lax.broadcasted_iota(jnp.int32, (Q_BLOCK, Q_KEYS), 1)

    def scores(qb, hh):
        q0 = qb * Q_BLOCK
        hs = slice(hh * ATT_HEAD_DIM, (hh + 1) * ATT_HEAD_DIM)
        s = _dot_nt(q_buf[q0:q0 + Q_BLOCK, hs], k_buf[q0:q0 + Q_KEYS, hs]) + bias_buf[hh]
        if q0 < PAST:
            s = jnp.where(col >= jnp.where(first, PAST - q0, 0), s, NEG_INF)
        return s

    def finish(qb, hh, s):
        q0 = qb * Q_BLOCK
        hs = slice(hh * ATT_HEAD_DIM, (hh + 1) * ATT_HEAD_DIM)
        e = jnp.exp2(s - jnp.max(s, axis=-1, keepdims=True))
        oe = _dot(e.astype(BF16), v_buf[q0:q0 + Q_KEYS, hh * V_SLOT:(hh + 1) * V_SLOT])
        o = oe[:, :ATT_HEAD_DIM] / oe[:, ATT_HEAD_DIM:]
        o_ref[q0:q0 + Q_BLOCK, hs] = (o * g_buf[q0:q0 + Q_BLOCK, hs]).astype(BF16)

    assert n_pieces == ATT_HEADS
    order = [(qb, hh) for qb in range(T // Q_BLOCK) for hh in range(ATT_HEADS)]
    for piece in range(n_pieces):
        project_piece(0, piece)
    s = scores(*order[0])
    for n, (qb, hh) in enumerate(order):
        if (qb + 1) * Q_BLOCK < T:
            project_piece(qb + 1, hh)
        s_next = scores(*order[n + 1]) if n + 1 < len(order) else None
        finish(qb, hh, s)
        s = s_next

    k_buf[0:PAST, :] = k_buf[T:T + PAST, :]
    for hh in range(ATT_HEADS):
        vs = slice(hh * V_SLOT, hh * V_SLOT + ATT_HEAD_DIM)
        v_buf[0:PAST, vs] = v_buf[T:T + PAST, vs]


def _att_branch(x, norm_g, w_att, q_norm_g, k_norm_g, bias_table):
    B, S, _ = x.shape
    return pl.pallas_call(
        _att_kernel,
        grid=(B, S // ATT_SEQ_BLOCK),
        in_specs=[_token_spec(D_MODEL, ATT_SEQ_BLOCK), _const_spec((1, D_MODEL)),
                  _const_spec((D_MODEL, 4 * D_ATT)),
                  _const_spec((1, ATT_HEAD_DIM)), _const_spec((1, ATT_HEAD_DIM)),
                  _const_spec((ATT_HEADS, BIAS_TABLE))],
        out_specs=_token_spec(D_ATT, ATT_SEQ_BLOCK),
        out_shape=jax.ShapeDtypeStruct((B, S, D_ATT), BF16),
        scratch_shapes=[pltpu.VMEM((ATT_SEQ_BLOCK, D_ATT), BF16),
                        pltpu.VMEM((PAST + ATT_SEQ_BLOCK, D_ATT), BF16),
                        pltpu.VMEM((PAST + ATT_SEQ_BLOCK, ATT_HEADS * V_SLOT), BF16),
                        pltpu.VMEM((ATT_SEQ_BLOCK, D_ATT), F32),
                        pltpu.VMEM((ATT_HEADS, Q_BLOCK, Q_KEYS), F32),
                        pltpu.VMEM((Q_BLOCK, D_MODEL), BF16)],
        compiler_params=_params(),
        name="att_branch",
    )(x, norm_g, w_att, q_norm_g, k_norm_g, bias_table)


def _bias_table(rel_bias):
    assert Q_KEYS - 1 - MAX_REL == BIAS_TABLE - (2 * MAX_REL + 1)
    return jnp.pad(rel_bias[:, ::-1], ((0, 0), (Q_KEYS - 1 - MAX_REL, 0)), mode="edge")


def _mem_kernel(x_ref, ng_ref, w_ref, qg_ref, k_ref, v_ref, o_ref):
    h = _rms_rows(x_ref[...], ng_ref[...]).astype(BF16)
    q_scale = MEM_HEAD_DIM ** -0.5 * LOG2E

    def head_slice(hh):
        return slice(hh * MEM_HEAD_DIM, (hh + 1) * MEM_HEAD_DIM)

    def q_head(hh):
        return (_rms_rows(_dot(h, w_ref[:, head_slice(hh)]), qg_ref[...]) * q_scale).astype(BF16)

    q = q_head(0)
    for hh in range(MEM_HEADS):
        hs = head_slice(hh)
        sg = jax.nn.silu(_dot(h, w_ref[:, D_MEM + hh * MEM_HEAD_DIM:D_MEM + (hh + 1) * MEM_HEAD_DIM]))
        q_next = q_head(hh + 1) if hh + 1 < MEM_HEADS else None
        s = _dot_nt(q, k_ref[:, hs])
        e = jnp.exp2(s - jnp.max(s, axis=-1, keepdims=True))
        l = jnp.sum(e, axis=-1, keepdims=True)
        o = _dot(e.astype(BF16), v_ref[:, hs]) / l
        o_ref[:, hs] = (o * sg).astype(BF16)
        q = q_next


def _mem_branch(x, norm_g, w_mem, mem_q_norm_g, k_mem, v_mem):
    B, S, _ = x.shape
    M = k_mem.shape[1]
    kv_spec = pl.BlockSpec((None, M, D_MEM), lambda b, s: (b, 0, 0))
    return pl.pallas_call(
        _mem_kernel,
        grid=(B, S // MEM_SEQ_BLOCK),
        in_specs=[_token_spec(D_MODEL, MEM_SEQ_BLOCK), _const_spec((1, D_MODEL)),
                  _const_spec((D_MODEL, 2 * D_MEM)), _const_spec((1, MEM_HEAD_DIM)), kv_spec, kv_spec],
        out_specs=_token_spec(D_MEM, MEM_SEQ_BLOCK),
        out_shape=jax.ShapeDtypeStruct((B, S, D_MEM), BF16),
        compiler_params=_params(),
        name="mem_branch",
    )(x, norm_g, w_mem, mem_q_norm_g, k_mem, v_mem)


def _rnn_out_kernel(x_ref, aa_ref, am_ref, ng_ref, p_ref, pt_ref, w_ref, cw_ref, cb_ref, wg_ref, bg_ref, lam_ref,
                    wmg_ref, bm_ref, wpr_ref, wpa_ref, wpm_ref, wo_ref, o_ref,
                    xr_buf, a_buf, b_buf, sg_buf, gate_buf, g0_buf, zp_buf, carry_ref,
                    h_buf, hp_buf, y_buf, z_buf):
    R = RNN_ROWS
    chunks = [slice(c * COL_CHUNK, (c + 1) * COL_CHUNK) for c in range(D_MODEL // COL_CHUNK)]

    @pl.when(pl.program_id(1) == 0)
    def _():
        xr_buf[:, 0:CONV_HIST, :] = jnp.zeros((RNN_GROUPS, CONV_HIST, D_RNN), F32)
        carry_ref[...] = jnp.zeros(carry_ref.shape, F32)

    neg_log_a = LRU_C * jax.nn.softplus(-lam_ref[...])
    a_exp = -LOG2E * neg_log_a

    def merge_gate(n, cs):
        gs = slice(n * D_MODEL + cs.start, n * D_MODEL + cs.stop)
        return jax.nn.sigmoid(_dot(h_buf[...], wmg_ref[:, gs]) + bm_ref[:, gs])

    for g in range(RNN_GROUPS):
        bs = slice(g * RNN_BATCH, (g + 1) * RNN_BATCH)
        xr_g = xr_buf.at[g]

        h_buf[...] = _rms_rows(x_ref[bs].reshape(R, D_MODEL), ng_ref[...]).astype(BF16)
        hp_buf[...] = _dot(p_ref[...], h_buf[...]).astype(BF16)
        for cs in chunks:
            xr_g[CONV_HIST:CONV_HIST + R, cs] = _dot(hp_buf[...], w_ref[:, cs])
            sg_buf[:, cs] = jax.nn.silu(_dot(hp_buf[...], w_ref[:, D_RNN + cs.start:D_RNN + cs.stop]))

        for cs in chunks:
            g0_buf[:, cs] = merge_gate(0, cs)

        for n in range(RNN_BLOCKS):
            cs = slice(n * RNN_BLOCK_DIM, (n + 1) * RNN_BLOCK_DIM)
            xc = cb_ref[:, cs]
            for j in range(CONV_WIDTH):
                xc = xc + cw_ref[j:j + 1, cs] * xr_g[j * RNN_BATCH:j * RNN_BATCH + R, cs]
            a_buf[:, cs] = xc
            gate_buf[:, 2 * n * RNN_BLOCK_DIM:2 * (n + 1) * RNN_BLOCK_DIM] = (
                _dot(xc.astype(BF16), wg_ref[n]) + bg_ref[n])
        xr_g[0:CONV_HIST, :] = xr_g[R:R + CONV_HIST, :]

        aa = aa_ref[bs].reshape(R, D_ATT)
        am = am_ref[bs].reshape(R, D_MEM)
        for cs in chunks:
            zp_buf[:, cs] = (merge_gate(1, cs) * _dot(aa, wpa_ref[:, cs])
                             + merge_gate(2, cs) * _dot(am, wpm_ref[:, cs]))

        for n in range(RNN_BLOCKS):
            cs = slice(n * RNN_BLOCK_DIM, (n + 1) * RNN_BLOCK_DIM)
            xc = a_buf[:, cs]
            r = jax.nn.sigmoid(gate_buf[:, 2 * n * RNN_BLOCK_DIM:(2 * n + 1) * RNN_BLOCK_DIM])
            i = jax.nn.sigmoid(gate_buf[:, (2 * n + 1) * RNN_BLOCK_DIM:(2 * n + 2) * RNN_BLOCK_DIM])
            a = jnp.exp2(r * a_exp[:, cs])
            one_minus_a2 = jnp.tanh(r * neg_log_a[:, cs]) * (1.0 + a * a)
            root = jnp.where(one_minus_a2 > 0.0, one_minus_a2 * lax.rsqrt(one_minus_a2), 0.0)
            a_buf[:, cs] = a
            b_buf[:, cs] = root * (i * xc)

        hc = carry_ref[g]
        for t in range(RNN_STEPS):
            rows = slice(t * RNN_BATCH, (t + 1) * RNN_BATCH)
            hc = a_buf[rows, :] * hc + b_buf[rows, :]
            b_buf[rows, :] = hc
        carry_ref[g] = hc

        for cs in chunks:
            y = (b_buf[:, cs] * sg_buf[:, cs]).astype(BF16)
            y_buf[:, cs] = _dot(pt_ref[...], y).astype(BF16)

        for cs in chunks:
            z_buf[:, cs] = (zp_buf[:, cs] + g0_buf[:, cs] * _dot(y_buf[...], wpr_ref[:, cs])).astype(BF16)
        for cs in chunks:
            out = x_ref[bs, :, cs].reshape(R, COL_CHUNK) + _dot(z_buf[...], wo_ref[:, cs])
            o_ref[bs, :, cs] = out.reshape(RNN_BATCH, RNN_STEPS, COL_CHUNK)


def _row_shuffle():
    out_row = np.arange(RNN_ROWS)
    t, b = out_row // RNN_BATCH, out_row % RNN_BATCH
    p = np.zeros((RNN_ROWS, RNN_ROWS), np.float32)
    p[out_row, b * RNN_STEPS + t] = 1.0
    return p


def _rnn_merge_out(x, a_att, a_mem, norm_g, w_rnn, conv_w, conv_b, w_gates, b_gates, lru_lambda,
                   w_merge, b_merge, w_proj_rnn, w_proj_att, w_proj_mem, w_out):
    B, S, _ = x.shape
    p = _row_shuffle()
    blk = lambda width: pl.BlockSpec((RNN_GROUPS * RNN_BATCH, RNN_STEPS, width), lambda b, s: (b, s, 0))
    sq = _const_spec((D_MODEL, D_MODEL))
    plane = lambda dtype, width=D_RNN: pltpu.VMEM((RNN_ROWS, width), dtype)
    return pl.pallas_call(
        _rnn_out_kernel,
        grid=(B // (RNN_GROUPS * RNN_BATCH), S // RNN_STEPS),
        in_specs=[blk(D_MODEL), blk(D_ATT), blk(D_MEM), _const_spec((1, D_MODEL)),
                  _const_spec((RNN_ROWS, RNN_ROWS)), _const_spec((RNN_ROWS, RNN_ROWS)),
                  _const_spec((D_MODEL, 2 * D_RNN)),
                  _const_spec((CONV_WIDTH, D_RNN)), _const_spec((1, D_RNN)),
                  _const_spec((RNN_BLOCKS, RNN_BLOCK_DIM, 2 * RNN_BLOCK_DIM)),
                  _const_spec((RNN_BLOCKS, 1, 2 * RNN_BLOCK_DIM)), _const_spec((1, D_RNN)),
                  _const_spec((D_MODEL, N_BRANCHES * D_MODEL)), _const_spec((1, N_BRANCHES * D_MODEL)),
                  sq, sq, sq, sq],
        out_specs=blk(D_MODEL),
        out_shape=jax.ShapeDtypeStruct((B, S, D_MODEL), F32),
        scratch_shapes=[pltpu.VMEM((RNN_GROUPS, CONV_HIST + RNN_ROWS, D_RNN), F32),
                        plane(F32), plane(F32), plane(F32), plane(F32, 2 * D_RNN), plane(F32), plane(F32),
                        pltpu.VMEM((RNN_GROUPS, RNN_BATCH, D_RNN), F32),
                        plane(BF16), plane(BF16), plane(BF16), plane(BF16)],
        compiler_params=_params(),
        name="rnn_merge_out",
    )(x, a_att, a_mem, norm_g, jnp.asarray(p, BF16), jnp.asarray(p.T, BF16), w_rnn, conv_w, conv_b, w_gates,
      b_gates, lru_lambda, w_merge, b_merge, w_proj_rnn, w_proj_att, w_proj_mem, w_out)


@jax.jit
def kernel(x, mem, norm_g, mem_norm_g, w_in, b_merge, conv_w, conv_b, lru_wa, lru_ba, lru_wx, lru_bx, lru_lambda, q_norm_g, k_norm_g, rel_bias, w_mem_kv, mem_q_norm_g, mem_k_norm_g, w_proj_rnn, w_proj_att, w_proj_mem, w_out):
    B, S, D = x.shape
    assert D == D_MODEL and S % ATT_SEQ_BLOCK == 0 and B % (RNN_GROUPS * RNN_BATCH) == 0 and w_in.shape[0] == 1

    def row(p):
        return p[0].reshape(1, -1)

    w = w_in[0].astype(BF16)
    o_att = 2 * D_RNN
    o_mem = o_att + 4 * D_ATT
    o_mrg = o_mem + 2 * D_MEM
    w_rnn, w_att, w_mem, w_mrg = w[:, :o_att], w[:, o_att:o_mem], w[:, o_mem:o_mrg], w[:, o_mrg:]
    w_gates = jnp.concatenate([lru_wa[0], lru_wx[0]], axis=-1).astype(BF16)
    b_gates = jnp.concatenate([lru_ba[0].reshape(RNN_BLOCKS, 1, RNN_BLOCK_DIM),
                               lru_bx[0].reshape(RNN_BLOCKS, 1, RNN_BLOCK_DIM)], axis=-1)
    ng = row(norm_g)

    k_mem, v_mem = _mem_kv(mem, row(mem_norm_g), w_mem_kv[0].astype(BF16), row(mem_k_norm_g))
    a_att = _att_branch(x, ng, w_att, row(q_norm_g), row(k_norm_g), _bias_table(rel_bias[0]))
    a_mem = _mem_branch(x, ng, w_mem, row(mem_q_norm_g), k_mem, v_mem)
    return _rnn_merge_out(x, a_att, a_mem, ng, w_rnn, conv_w[0], row(conv_b), w_gates, b_gates, row(lru_lambda),
                          w_mrg, row(b_merge), w_proj_rnn[0].astype(BF16), w_proj_att[0].astype(BF16),
                          w_proj_mem[0].astype(BF16), w_out[0].astype(BF16))
```

```python
import jax
import jax.numpy as jnp
import numpy as np
from jax import lax
from jax.experimental import pallas as pl
from jax.experimental.pallas import tpu as pltpu

D_MODEL = 1024
CHUNK = 64
N_PAST_CHUNKS = 8
PAST = N_PAST_CHUNKS * CHUNK
BAND = PAST + CHUNK
D_RNN = 1024
RNN_BLOCKS = 8
RNN_BLOCK_DIM = D_RNN // RNN_BLOCKS
CONV_WIDTH = 4
LRU_C = 8.0
ATT_HEADS = 8
ATT_HEAD_DIM = 128
D_ATT = ATT_HEADS * ATT_HEAD_DIM
MAX_REL = 256
MEM_HEADS = 4
MEM_HEAD_DIM = 256
D_MEM = MEM_HEADS * MEM_HEAD_DIM
N_BRANCHES = 3
EPS = 1e-6
NEG_INF = -1e30

V7X_SUBLANES = 8
V7X_MXU_DIM = 256
ATT_SEQ_BLOCK = 1024
MEM_SEQ_BLOCK = 1024
Q_BLOCK = 256
Q_KEYS = PAST + Q_BLOCK
COL_CHUNK = V7X_MXU_DIM
VMEM_LIMIT_BYTES = 56 * 1024 * 1024
RNN_BATCH = V7X_SUBLANES
RNN_STEPS = 32
RNN_ROWS = RNN_BATCH * RNN_STEPS
RNN_GROUPS = 2
CONV_HIST = (CONV_WIDTH - 1) * RNN_BATCH
BIAS_TABLE = 2 * PAST
V_SLOT = 2 * ATT_HEAD_DIM
LOG2E = float(np.log2(np.e))

BF16 = jnp.bfloat16
F32 = jnp.float32


def _dot(a, b):
    return jnp.dot(a, b, preferred_element_type=F32)


def _dot_nt(a, b):
    return lax.dot_general(a, b, (((1,), (1,)), ((), ())), preferred_element_type=F32)


def _rms_rows(xf, g):
    ms = jnp.mean(xf * xf, axis=-1, keepdims=True)
    return xf * lax.rsqrt(ms + EPS) * g


def _const_spec(shape):
    nd = len(shape)
    return pl.BlockSpec(shape, lambda *_: (0,) * nd, pipeline_mode=pl.Buffered(1))


def _token_spec(width, rows):
    return pl.BlockSpec((None, rows, width), lambda b, s: (b, s, 0))


def _params():
    return pltpu.CompilerParams(
        dimension_semantics=("arbitrary", "arbitrary"),
        vmem_limit_bytes=VMEM_LIMIT_BYTES,
    )


def _mem_kv_kernel(mem_ref, g_ref, w_ref, kg_ref, k_ref, v_ref):
    mn = _rms_rows(mem_ref[...], g_ref[...]).astype(BF16)
    for h in range(MEM_HEADS):
        c0 = h * MEM_HEAD_DIM
        km = _dot(mn, w_ref[:, c0:c0 + MEM_HEAD_DIM])
        k_ref[c0:c0 + MEM_HEAD_DIM, :] = _rms_rows(km, kg_ref[...]).T.astype(BF16)
        vm = _dot(mn, w_ref[:, D_MEM + c0:D_MEM + c0 + MEM_HEAD_DIM])
        v_ref[:, c0:c0 + MEM_HEAD_DIM] = vm.astype(BF16)


def _mem_kv(mem, mem_norm_g, w_mem_kv, mem_k_norm_g):
    B, M, _ = mem.shape
    blk = pl.BlockSpec((None, M, D_MODEL), lambda b: (b, 0, 0))
    return pl.pallas_call(
        _mem_kv_kernel,
        grid=(B,),
        in_specs=[blk, _const_spec((1, D_MODEL)), _const_spec((D_MODEL, 2 * D_MEM)),
                  _const_spec((1, MEM_HEAD_DIM))],
        out_specs=[pl.BlockSpec((None, D_MEM, M), lambda b: (b, 0, 0)),
                   pl.BlockSpec((None, M, D_MEM), lambda b: (b, 0, 0))],
        out_shape=[jax.ShapeDtypeStruct((B, D_MEM, M), BF16), jax.ShapeDtypeStruct((B, M, D_MEM), BF16)],
        compiler_params=pltpu.CompilerParams(dimension_semantics=("arbitrary",),
                                             vmem_limit_bytes=VMEM_LIMIT_BYTES),
        name="mem_kv",
    )(mem, mem_norm_g, w_mem_kv, mem_k_norm_g)


def _att_kernel(x_ref, ng_ref, w_ref, qg_ref, kg_ref, tbl_ref, o_ref,
                q_buf, kt_buf, v_buf, g_buf, bias_buf, h_buf):
    T = ATT_SEQ_BLOCK
    first = pl.program_id(1) == 0

    @pl.when(jnp.logical_and(first, pl.program_id(0) == 0))
    def _():
        i = lax.broadcasted_iota(jnp.int32, (Q_BLOCK, Q_KEYS), 0)
        j = lax.broadcasted_iota(jnp.int32, (Q_BLOCK, Q_KEYS), 1)
        off = j - (i // CHUNK) * CHUNK
        in_band = jnp.logical_and(off >= 0, off < BAND)
        for hh in range(ATT_HEADS):
            row = jnp.broadcast_to(tbl_ref[hh:hh + 1, :], (Q_BLOCK, BIAS_TABLE))
            win = pltpu.roll(row, BIAS_TABLE - (Q_BLOCK - 1), 1, stride=1, stride_axis=0)
            bias_buf[hh] = jnp.where(in_band, win[:, :Q_KEYS] * LOG2E, NEG_INF)
        v_buf[...] = jnp.ones(v_buf.shape, BF16)

    @pl.when(first)
    def _():
        kt_buf[:, 0:PAST] = jnp.zeros((D_ATT, PAST), BF16)
        for hh in range(ATT_HEADS):
            v_buf[0:PAST, hh * V_SLOT:hh * V_SLOT + ATT_HEAD_DIM] = jnp.zeros((PAST, ATT_HEAD_DIM), BF16)

    q_scale = ATT_HEAD_DIM ** -0.5 * LOG2E
    heads_per_chunk = COL_CHUNK // ATT_HEAD_DIM
    n_pieces = 2 * (D_ATT // COL_CHUNK)

    def project_piece(qb, piece):
        q0 = qb * Q_BLOCK
        rows = slice(q0, q0 + Q_BLOCK)
        new = slice(PAST + q0, PAST + q0 + Q_BLOCK)
        c, second_half = divmod(piece, 2)
        c0 = c * COL_CHUNK
        if piece == 0:
            h_buf[...] = _rms_rows(x_ref[rows, :], ng_ref[...]).astype(BF16)
        h = h_buf[...]
        if not second_half:
            q = _dot(h, w_ref[:, c0:c0 + COL_CHUNK])
            k = _dot(h, w_ref[:, D_ATT + c0:D_ATT + c0 + COL_CHUNK])
            for hh in range(heads_per_chunk):
                hs = slice(hh * ATT_HEAD_DIM, (hh + 1) * ATT_HEAD_DIM)
                os_ = slice(c0 + hh * ATT_HEAD_DIM, c0 + (hh + 1) * ATT_HEAD_DIM)
                q_buf[rows, os_] = (_rms_rows(q[:, hs], qg_ref[...]) * q_scale).astype(BF16)
                kt_buf[os_, new] = _rms_rows(k[:, hs], kg_ref[...]).T.astype(BF16)
        else:
            v = _dot(h, w_ref[:, 2 * D_ATT + c0:2 * D_ATT + c0 + COL_CHUNK])
            for hh in range(heads_per_chunk):
                hs = slice(hh * ATT_HEAD_DIM, (hh + 1) * ATT_HEAD_DIM)
                head = c * heads_per_chunk + hh
                v_buf[new, head * V_SLOT:head * V_SLOT + ATT_HEAD_DIM] = v[:, hs].astype(BF16)
            g_buf[rows, c0:c0 + COL_CHUNK] = jax.nn.silu(
                _dot(h, w_ref[:, 3 * D_ATT + c0:3 * D_ATT + c0 + COL_CHUNK]))

    col = lax.broadcasted_iota(jnp.int32, (Q_BLOCK, Q_KEYS), 1)

    def scores(qb, hh):
        q0 = qb * Q_BLOCK
        hs = slice(hh * ATT_HEAD_DIM, (hh + 1) * ATT_HEAD_DIM)
        s = _dot(q_buf[q0:q0 + Q_BLOCK, hs], kt_buf[hs, q0:q0 + Q_KEYS]) + bias_buf[hh]
        if q0 < PAST:
            s = jnp.where(col >= jnp.where(first, PAST - q0, 0), s, NEG_INF)
        return s

    def finish(qb, hh, s):
        q0 = qb * Q_BLOCK
        hs = slice(hh * ATT_HEAD_DIM, (hh + 1) * ATT_HEAD_DIM)
        e = jnp.exp2(s - jnp.max(s, axis=-1, keepdims=True))
        oe = _dot(e.astype(BF16), v_buf[q0:q0 + Q_KEYS, hh * V_SLOT:(hh + 1) * V_SLOT])
        o = oe[:, :ATT_HEAD_DIM] / oe[:, ATT_HEAD_DIM:]
        o_ref[q0:q0 + Q_BLOCK, hs] = (o * g_buf[q0:q0 + Q_BLOCK, hs]).astype(BF16)

    assert n_pieces == ATT_HEADS
    order = [(qb, hh) for qb in range(T // Q_BLOCK) for hh in range(ATT_HEADS)]
    for piece in range(n_pieces):
        project_piece(0, piece)
    s = scores(*order[0])
    for n, (qb, hh) in enumerate(order):
        if (qb + 1) * Q_BLOCK < T:
            project_piece(qb + 1, hh)
        s_next = scores(*order[n + 1]) if n + 1 < len(order) else None
        finish(qb, hh, s)
        s = s_next

    kt_buf[:, 0:PAST] = kt_buf[:, T:T + PAST]
    for hh in range(ATT_HEADS):
        vs = slice(hh * V_SLOT, hh * V_SLOT + ATT_HEAD_DIM)
        v_buf[0:PAST, vs] = v_buf[T:T + PAST, vs]


def _att_branch(x, norm_g, w_att, q_norm_g, k_norm_g, bias_table):
    B, S, _ = x.shape
    return pl.pallas_call(
        _att_kernel,
        grid=(B, S // ATT_SEQ_BLOCK),
        in_specs=[_token_spec(D_MODEL, ATT_SEQ_BLOCK), _const_spec((1, D_MODEL)),
                  _const_spec((D_MODEL, 4 * D_ATT)),
                  _const_spec((1, ATT_HEAD_DIM)), _const_spec((1, ATT_HEAD_DIM)),
                  _const_spec((ATT_HEADS, BIAS_TABLE))],
        out_specs=_token_spec(D_ATT, ATT_SEQ_BLOCK),
        out_shape=jax.ShapeDtypeStruct((B, S, D_ATT), BF16),
        scratch_shapes=[pltpu.VMEM((ATT_SEQ_BLOCK, D_ATT), BF16),
                        pltpu.VMEM((D_ATT, PAST + ATT_SEQ_BLOCK), BF16),
                        pltpu.VMEM((PAST + ATT_SEQ_BLOCK, ATT_HEADS * V_SLOT), BF16),
                        pltpu.VMEM((ATT_SEQ_BLOCK, D_ATT), F32),
                        pltpu.VMEM((ATT_HEADS, Q_BLOCK, Q_KEYS), F32),
                        pltpu.VMEM((Q_BLOCK, D_MODEL), BF16)],
        compiler_params=_params(),
        name="att_branch",
    )(x, norm_g, w_att, q_norm_g, k_norm_g, bias_table)


def _bias_table(rel_bias):
    assert Q_KEYS - 1 - MAX_REL == BIAS_TABLE - (2 * MAX_REL + 1)
    return jnp.pad(rel_bias[:, ::-1], ((0, 0), (Q_KEYS - 1 - MAX_REL, 0)), mode="edge")


def _mem_kernel(x_ref, ng_ref, w_ref, qg_ref, kt_ref, v_ref, o_ref):
    h = _rms_rows(x_ref[...], ng_ref[...]).astype(BF16)
    q_scale = MEM_HEAD_DIM ** -0.5 * LOG2E

    def head_slice(hh):
        return slice(hh * MEM_HEAD_DIM, (hh + 1) * MEM_HEAD_DIM)

    def q_head(hh):
        return (_rms_rows(_dot(h, w_ref[:, head_slice(hh)]), qg_ref[...]) * q_scale).astype(BF16)

    q = q_head(0)
    for hh in range(MEM_HEADS):
        hs = head_slice(hh)
        sg = jax.nn.silu(_dot(h, w_ref[:, D_MEM + hh * MEM_HEAD_DIM:D_MEM + (hh + 1) * MEM_HEAD_DIM]))
        q_next = q_head(hh + 1) if hh + 1 < MEM_HEADS else None
        s = _dot(q, kt_ref[hs, :])
        e = jnp.exp2(s - jnp.max(s, axis=-1, keepdims=True))
        l = jnp.sum(e, axis=-1, keepdims=True)
        o = _dot(e.astype(BF16), v_ref[:, hs]) / l
        o_ref[:, hs] = (o * sg).astype(BF16)
        q = q_next


def _mem_branch(x, norm_g, w_mem, mem_q_norm_g, kt_mem, v_mem):
    B, S, _ = x.shape
    M = v_mem.shape[1]
    kt_spec = pl.BlockSpec((None, D_MEM, M), lambda b, s: (b, 0, 0))
    v_spec = pl.BlockSpec((None, M, D_MEM), lambda b, s: (b, 0, 0))
    return pl.pallas_call(
        _mem_kernel,
        grid=(B, S // MEM_SEQ_BLOCK),
        in_specs=[_token_spec(D_MODEL, MEM_SEQ_BLOCK), _const_spec((1, D_MODEL)),
                  _const_spec((D_MODEL, 2 * D_MEM)), _const_spec((1, MEM_HEAD_DIM)), kt_spec, v_spec],
        out_specs=_token_spec(D_MEM, MEM_SEQ_BLOCK),
        out_shape=jax.ShapeDtypeStruct((B, S, D_MEM), BF16),
        compiler_params=_params(),
        name="mem_branch",
    )(x, norm_g, w_mem, mem_q_norm_g, kt_mem, v_mem)


def _rnn_out_kernel(x_ref, aa_ref, am_ref, ng_ref, p_ref, pt_ref, w_ref, cw_ref, cb_ref, wg_ref, bg_ref, lam_ref,
                    wmg_ref, bm_ref, wpr_ref, wpa_ref, wpm_ref, wo_ref, o_ref,
                    xr_buf, a_buf, b_buf, sg_buf, gate_buf, g0_buf, zp_buf, carry_ref,
                    h_buf, hp_buf, y_buf, z_buf):
    R = RNN_ROWS
    chunks = [slice(c * COL_CHUNK, (c + 1) * COL_CHUNK) for c in range(D_MODEL // COL_CHUNK)]

    @pl.when(pl.program_id(1) == 0)
    def _():
        xr_buf[:, 0:CONV_HIST, :] = jnp.zeros((RNN_GROUPS, CONV_HIST, D_RNN), F32)
        carry_ref[...] = jnp.zeros(carry_ref.shape, F32)

    neg_log_a = LRU_C * jax.nn.softplus(-lam_ref[...])
    a_exp = -LOG2E * neg_log_a

    def merge_gate(n, cs):
        gs = slice(n * D_MODEL + cs.start, n * D_MODEL + cs.stop)
        return jax.nn.sigmoid(_dot(h_buf[...], wmg_ref[:, gs]) + bm_ref[:, gs])

    for g in range(RNN_GROUPS):
        bs = slice(g * RNN_BATCH, (g + 1) * RNN_BATCH)
        xr_g = xr_buf.at[g]

        h_buf[...] = _rms_rows(x_ref[bs].reshape(R, D_MODEL), ng_ref[...]).astype(BF16)
        hp_buf[...] = _dot(p_ref[...], h_buf[...]).astype(BF16)
        for cs in chunks:
            xr_g[CONV_HIST:CONV_HIST + R, cs] = _dot(hp_buf[...], w_ref[:, cs])
            sg_buf[:, cs] = jax.nn.silu(_dot(hp_buf[...], w_ref[:, D_RNN + cs.start:D_RNN + cs.stop]))

        for cs in chunks:
            g0_buf[:, cs] = merge_gate(0, cs)

        for n in range(RNN_BLOCKS):
            cs = slice(n * RNN_BLOCK_DIM, (n + 1) * RNN_BLOCK_DIM)
            xc = cb_ref[:, cs]
            for j in range(CONV_WIDTH):
                xc = xc + cw_ref[j:j + 1, cs] * xr_g[j * RNN_BATCH:j * RNN_BATCH + R, cs]
            a_buf[:, cs] = xc
            gate_buf[:, 2 * n * RNN_BLOCK_DIM:2 * (n + 1) * RNN_BLOCK_DIM] = (
                _dot(xc.astype(BF16), wg_ref[n]) + bg_ref[n])
        xr_g[0:CONV_HIST, :] = xr_g[R:R + CONV_HIST, :]

        aa = aa_ref[bs].reshape(R, D_ATT)
        am = am_ref[bs].reshape(R, D_MEM)
        for cs in chunks:
            zp_buf[:, cs] = (merge_gate(1, cs) * _dot(aa, wpa_ref[:, cs])
                             + merge_gate(2, cs) * _dot(am, wpm_ref[:, cs]))

        for n in range(RNN_BLOCKS):
            cs = slice(n * RNN_BLOCK_DIM, (n + 1) * RNN_BLOCK_DIM)
            xc = a_buf[:, cs]
            r = jax.nn.sigmoid(gate_buf[:, 2 * n * RNN_BLOCK_DIM:(2 * n + 1) * RNN_BLOCK_DIM])
            i = jax.nn.sigmoid(gate_buf[:, (2 * n + 1) * RNN_BLOCK_DIM:(2 * n + 2) * RNN_BLOCK_DIM])
            a = jnp.exp2(r * a_exp[:, cs])
            one_minus_a2 = jnp.tanh(r * neg_log_a[:, cs]) * (1.0 + a * a)
            root = jnp.where(one_minus_a2 > 0.0, one_minus_a2 * lax.rsqrt(one_minus_a2), 0.0)
            a_buf[:, cs] = a
            b_buf[:, cs] = root * (i * xc)

        hc = carry_ref[g]
        for t in range(RNN_STEPS):
            rows = slice(t * RNN_BATCH, (t + 1) * RNN_BATCH)
            hc = a_buf[rows, :] * hc + b_buf[rows, :]
            b_buf[rows, :] = hc
        carry_ref[g] = hc

        for cs in chunks:
            y = (b_buf[:, cs] * sg_buf[:, cs]).astype(BF16)
            y_buf[:, cs] = _dot(pt_ref[...], y).astype(BF16)

        for cs in chunks:
            z_buf[:, cs] = (zp_buf[:, cs] + g0_buf[:, cs] * _dot(y_buf[...], wpr_ref[:, cs])).astype(BF16)
        for cs in chunks:
            out = x_ref[bs, :, cs].reshape(R, COL_CHUNK) + _dot(z_buf[...], wo_ref[:, cs])
            o_ref[bs, :, cs] = out.reshape(RNN_BATCH, RNN_STEPS, COL_CHUNK)


def _row_shuffle():
    out_row = np.arange(RNN_ROWS)
    t, b = out_row // RNN_BATCH, out_row % RNN_BATCH
    p = np.zeros((RNN_ROWS, RNN_ROWS), np.float32)
    p[out_row, b * RNN_STEPS + t] = 1.0
    return p


def _rnn_merge_out(x, a_att, a_mem, norm_g, w_rnn, conv_w, conv_b, w_gates, b_gates, lru_lambda,
                   w_merge, b_merge, w_proj_rnn, w_proj_att, w_proj_mem, w_out):
    B, S, _ = x.shape
    p = _row_shuffle()
    blk = lambda width: pl.BlockSpec((RNN_GROUPS * RNN_BATCH, RNN_STEPS, width), lambda b, s: (b, s, 0))
    sq = _const_spec((D_MODEL, D_MODEL))
    plane = lambda dtype, width=D_RNN: pltpu.VMEM((RNN_ROWS, width), dtype)
    return pl.pallas_call(
        _rnn_out_kernel,
        grid=(B // (RNN_GROUPS * RNN_BATCH), S // RNN_STEPS),
        in_specs=[blk(D_MODEL), blk(D_ATT), blk(D_MEM), _const_spec((1, D_MODEL)),
                  _const_spec((RNN_ROWS, RNN_ROWS)), _const_spec((RNN_ROWS, RNN_ROWS)),
                  _const_spec((D_MODEL, 2 * D_RNN)),
                  _const_spec((CONV_WIDTH, D_RNN)), _const_spec((1, D_RNN)),
                  _const_spec((RNN_BLOCKS, RNN_BLOCK_DIM, 2 * RNN_BLOCK_DIM)),
                  _const_spec((RNN_BLOCKS, 1, 2 * RNN_BLOCK_DIM)), _const_spec((1, D_RNN)),
                  _const_spec((D_MODEL, N_BRANCHES * D_MODEL)), _const_spec((1, N_BRANCHES * D_MODEL)),
                  sq, sq, sq, sq],
        out_specs=blk(D_MODEL),
        out_shape=jax.ShapeDtypeStruct((B, S, D_MODEL), F32),
        scratch_shapes=[pltpu.VMEM((RNN_GROUPS, CONV_HIST + RNN_ROWS, D_RNN), F32),
                        plane(F32), plane(F32), plane(F32), plane(F32, 2 * D_RNN), plane(F32), plane(F32),
                        pltpu.VMEM((RNN_GROUPS, RNN_BATCH, D_RNN), F32),
                        plane(BF16), plane(BF16), plane(BF16), plane(BF16)],
        compiler_params=_params(),
        name="rnn_merge_out",
    )(x, a_att, a_mem, norm_g, jnp.asarray(p, BF16), jnp.asarray(p.T, BF16), w_rnn, conv_w, conv_b, w_gates,
      b_gates, lru_lambda, w_merge, b_merge, w_proj_rnn, w_proj_att, w_proj_mem, w_out)


@jax.jit
def kernel(x, mem, norm_g, mem_norm_g, w_in, b_merge, conv_w, conv_b, lru_wa, lru_ba, lru_wx, lru_bx, lru_lambda, q_norm_g, k_norm_g, rel_bias, w_mem_kv, mem_q_norm_g, mem_k_norm_g, w_proj_rnn, w_proj_att, w_proj_mem, w_out):
    B, S, D = x.shape
    assert D == D_MODEL and S % ATT_SEQ_BLOCK == 0 and B % (RNN_GROUPS * RNN_BATCH) == 0 and w_in.shape[0] == 1

    def row(p):
        return p[0].reshape(1, -1)

    w = w_in[0].astype(BF16)
    o_att = 2 * D_RNN
    o_mem = o_att + 4 * D_ATT
    o_mrg = o_mem + 2 * D_MEM
    w_rnn, w_att, w_mem, w_mrg = w[:, :o_att], w[:, o_att:o_mem], w[:, o_mem:o_mrg], w[:, o_mrg:]
    w_gates = jnp.concatenate([lru_wa[0], lru_wx[0]], axis=-1).astype(BF16)
    b_gates = jnp.concatenate([lru_ba[0].reshape(RNN_BLOCKS, 1, RNN_BLOCK_DIM),
                               lru_bx[0].reshape(RNN_BLOCKS, 1, RNN_BLOCK_DIM)], axis=-1)
    ng = row(norm_g)

    k_mem, v_mem = _mem_kv(mem, row(mem_norm_g), w_mem_kv[0].astype(BF16), row(mem_k_norm_g))
    a_att = _att_branch(x, ng, w_att, row(q_norm_g), row(k_norm_g), _bias_table(rel_bias[0]))
    a_mem = _mem_branch(x, ng, w_mem, row(mem_q_norm_g), k_mem, v_mem)
    return _rnn_merge_out(x, a_att, a_mem, ng, w_rnn, conv_w[0], row(conv_b), w_gates, b_gates, row(lru_lambda),
                          w_mrg, row(b_merge), w_proj_rnn[0].astype(BF16), w_proj_att[0].astype(BF16),
                          w_proj_mem[0].astype(BF16), w_out[0].astype(BF16))
```

```python
import jax
import jax.numpy as jnp
import numpy as np
from jax import lax
from jax.experimental import pallas as pl
from jax.experimental.pallas import tpu as pltpu

D_MODEL = 1024
CHUNK = 64
N_PAST_CHUNKS = 8
PAST = N_PAST_CHUNKS * CHUNK
BAND = PAST + CHUNK
D_RNN = 1024
RNN_BLOCKS = 8
RNN_BLOCK_DIM = D_RNN // RNN_BLOCKS
CONV_WIDTH = 4
LRU_C = 8.0
ATT_HEADS = 8
ATT_HEAD_DIM = 128
D_ATT = ATT_HEADS * ATT_HEAD_DIM
MAX_REL = 256
MEM_HEADS = 4
MEM_HEAD_DIM = 256
D_MEM = MEM_HEADS * MEM_HEAD_DIM
N_BRANCHES = 3
W_IN_ATT_COL = 2 * D_RNN
W_IN_MEM_COL = W_IN_ATT_COL + 4 * D_ATT
W_IN_MERGE_COL = W_IN_MEM_COL + 2 * D_MEM
EPS = 1e-6
NEG_INF = -1e30

V7X_SUBLANES = 8
V7X_MXU_DIM = 256
ATT_SEQ_BLOCK = 1024
MEM_SEQ_BLOCK = 1024
Q_BLOCK = 128
Q_KEYS = PAST + Q_BLOCK
PROJ_BLOCK = 256
COL_CHUNK = V7X_MXU_DIM
VMEM_LIMIT_BYTES = 56 * 1024 * 1024
RNN_BATCH = V7X_SUBLANES
RNN_STEPS = 32
RNN_ROWS = RNN_BATCH * RNN_STEPS
RNN_GROUPS = 2
CONV_HIST = (CONV_WIDTH - 1) * RNN_BATCH
BIAS_TABLE = Q_KEYS + Q_BLOCK
V_SLOT = 2 * ATT_HEAD_DIM
LOG2E = float(np.log2(np.e))

BF16 = jnp.bfloat16
F32 = jnp.float32


def _dot(a, b):
    return jnp.dot(a, b, preferred_element_type=F32)


def _dot_nt(a, b):
    return lax.dot_general(a, b, (((1,), (1,)), ((), ())), preferred_element_type=F32)


def _rms_rows(xf, g):
    ms = jnp.mean(xf * xf, axis=-1, keepdims=True)
    return xf * lax.rsqrt(ms + EPS) * g


def _const_spec(shape):
    nd = len(shape)
    return pl.BlockSpec(shape, lambda *_: (0,) * nd, pipeline_mode=pl.Buffered(1))


def _w_in_spec(width, col_block):
    return pl.BlockSpec((D_MODEL, width), lambda *_: (0, col_block), pipeline_mode=pl.Buffered(1))


def _token_spec(width, rows):
    return pl.BlockSpec((None, rows, width), lambda b, s: (b, s, 0))


def _params():
    return pltpu.CompilerParams(
        dimension_semantics=("arbitrary", "arbitrary"),
        vmem_limit_bytes=VMEM_LIMIT_BYTES,
    )


def _mem_kv_kernel(mem_ref, g_ref, w_ref, kg_ref, k_ref, v_ref):
    mn = _rms_rows(mem_ref[...], g_ref[...]).astype(BF16)
    for h in range(MEM_HEADS):
        c0 = h * MEM_HEAD_DIM
        km = _dot(mn, w_ref[:, c0:c0 + MEM_HEAD_DIM])
        k_ref[c0:c0 + MEM_HEAD_DIM, :] = _rms_rows(km, kg_ref[...]).T.astype(BF16)
        vm = _dot(mn, w_ref[:, D_MEM + c0:D_MEM + c0 + MEM_HEAD_DIM])
        v_ref[:, c0:c0 + MEM_HEAD_DIM] = vm.astype(BF16)


def _mem_kv(mem, mem_norm_g, w_mem_kv, mem_k_norm_g):
    B, M, _ = mem.shape
    blk = pl.BlockSpec((None, M, D_MODEL), lambda b: (b, 0, 0))
    return pl.pallas_call(
        _mem_kv_kernel,
        grid=(B,),
        in_specs=[blk, _const_spec((1, D_MODEL)), _const_spec((D_MODEL, 2 * D_MEM)),
                  _const_spec((1, MEM_HEAD_DIM))],
        out_specs=[pl.BlockSpec((None, D_MEM, M), lambda b: (b, 0, 0)),
                   pl.BlockSpec((None, M, D_MEM), lambda b: (b, 0, 0))],
        out_shape=[jax.ShapeDtypeStruct((B, D_MEM, M), BF16), jax.ShapeDtypeStruct((B, M, D_MEM), BF16)],
        compiler_params=pltpu.CompilerParams(dimension_semantics=("arbitrary",),
                                             vmem_limit_bytes=VMEM_LIMIT_BYTES),
        name="mem_kv",
    )(mem, mem_norm_g, w_mem_kv, mem_k_norm_g)


def _att_kernel(x_ref, ng_ref, wqk_ref, wvg_ref, qg_ref, kg_ref, tbl_ref, o_ref,
                q_buf, kt_buf, v_buf, g_buf, bias_buf, h_buf):
    T = ATT_SEQ_BLOCK
    first = pl.program_id(1) == 0

    @pl.when(jnp.logical_and(first, pl.program_id(0) == 0))
    def _():
        i = lax.broadcasted_iota(jnp.int32, (Q_BLOCK, Q_KEYS), 0)
        j = lax.broadcasted_iota(jnp.int32, (Q_BLOCK, Q_KEYS), 1)
        off = j - (i // CHUNK) * CHUNK
        in_band = jnp.logical_and(off >= 0, off < BAND)
        for hh in range(ATT_HEADS):
            row = jnp.broadcast_to(tbl_ref[hh:hh + 1, :], (Q_BLOCK, BIAS_TABLE))
            win = pltpu.roll(row, BIAS_TABLE - (Q_BLOCK - 1), 1, stride=1, stride_axis=0)
            bias_buf[hh] = jnp.where(in_band, win[:, :Q_KEYS] * LOG2E, NEG_INF)
        v_buf[...] = jnp.ones(v_buf.shape, BF16)

    @pl.when(first)
    def _():
        kt_buf[:, 0:PAST] = jnp.zeros((D_ATT, PAST), BF16)
        for hh in range(ATT_HEADS):
            v_buf[0:PAST, hh * V_SLOT:hh * V_SLOT + ATT_HEAD_DIM] = jnp.zeros((PAST, ATT_HEAD_DIM), BF16)

    q_scale = ATT_HEAD_DIM ** -0.5 * LOG2E
    heads_per_chunk = COL_CHUNK // ATT_HEAD_DIM
    n_pieces = 2 * (D_ATT // COL_CHUNK)

    def project_piece(qb, piece):
        q0 = qb * PROJ_BLOCK
        rows = slice(q0, q0 + PROJ_BLOCK)
        new = slice(PAST + q0, PAST + q0 + PROJ_BLOCK)
        c, second_half = divmod(piece, 2)
        c0 = c * COL_CHUNK
        if piece == 0:
            h_buf[...] = _rms_rows(x_ref[rows, :], ng_ref[...]).astype(BF16)
        h = h_buf[...]
        if not second_half:
            q = _dot(h, wqk_ref[:, c0:c0 + COL_CHUNK])
            k = _dot(h, wqk_ref[:, D_ATT + c0:D_ATT + c0 + COL_CHUNK])
            for hh in range(heads_per_chunk):
                hs = slice(hh * ATT_HEAD_DIM, (hh + 1) * ATT_HEAD_DIM)
                os_ = slice(c0 + hh * ATT_HEAD_DIM, c0 + (hh + 1) * ATT_HEAD_DIM)
                q_buf[rows, os_] = (_rms_rows(q[:, hs], qg_ref[...]) * q_scale).astype(BF16)
                kt_buf[os_, new] = _rms_rows(k[:, hs], kg_ref[...]).T.astype(BF16)
        else:
            v = _dot(h, wvg_ref[:, c0:c0 + COL_CHUNK])
            for hh in range(heads_per_chunk):
                hs = slice(hh * ATT_HEAD_DIM, (hh + 1) * ATT_HEAD_DIM)
                head = c * heads_per_chunk + hh
                v_buf[new, head * V_SLOT:head * V_SLOT + ATT_HEAD_DIM] = v[:, hs].astype(BF16)
            g_buf[rows, c0:c0 + COL_CHUNK] = jax.nn.silu(
                _dot(h, wvg_ref[:, D_ATT + c0:D_ATT + c0 + COL_CHUNK]))

    col = lax.broadcasted_iota(jnp.int32, (Q_BLOCK, Q_KEYS), 1)

    def scores(qb, hh):
        q0 = qb * Q_BLOCK
        hs = slice(hh * ATT_HEAD_DIM, (hh + 1) * ATT_HEAD_DIM)
        s = _dot(q_buf[q0:q0 + Q_BLOCK, hs], kt_buf[hs, q0:q0 + Q_KEYS]) + bias_buf[hh]
        if q0 < PAST:
            s = jnp.where(col >= jnp.where(first, PAST - q0, 0), s, NEG_INF)
        return s

    def finish(qb, hh, s):
        q0 = qb * Q_BLOCK
        hs = slice(hh * ATT_HEAD_DIM, (hh + 1) * ATT_HEAD_DIM)
        e = jnp.exp2(s - jnp.max(s, axis=-1, keepdims=True))
        oe = _dot(e.astype(BF16), v_buf[q0:q0 + Q_KEYS, hh * V_SLOT:(hh + 1) * V_SLOT])
        o = oe[:, :ATT_HEAD_DIM] / oe[:, ATT_HEAD_DIM:]
        o_ref[q0:q0 + Q_BLOCK, hs] = (o * g_buf[q0:q0 + Q_BLOCK, hs]).astype(BF16)

    order = [(qb, hh) for qb in range(T // Q_BLOCK) for hh in range(ATT_HEADS)]
    per_unit = (PROJ_BLOCK // Q_BLOCK) * ATT_HEADS
    assert per_unit % n_pieces == 0
    every = per_unit // n_pieces
    for piece in range(n_pieces):
        project_piece(0, piece)
    s = scores(*order[0])
    for n, (qb, hh) in enumerate(order):
        unit, in_unit = divmod(n, per_unit)
        if (unit + 1) * PROJ_BLOCK < T and in_unit % every == 0:
            project_piece(unit + 1, in_unit // every)
        s_next = scores(*order[n + 1]) if n + 1 < len(order) else None
        finish(qb, hh, s)
        s = s_next

    kt_buf[:, 0:PAST] = kt_buf[:, T:T + PAST]
    for hh in range(ATT_HEADS):
        vs = slice(hh * V_SLOT, hh * V_SLOT + ATT_HEAD_DIM)
        v_buf[0:PAST, vs] = v_buf[T:T + PAST, vs]


def _att_branch(x, norm_g, w_in, q_norm_g, k_norm_g, bias_table):
    B, S, _ = x.shape
    assert W_IN_ATT_COL % (2 * D_ATT) == 0
    qk_block = W_IN_ATT_COL // (2 * D_ATT)
    return pl.pallas_call(
        _att_kernel,
        grid=(B, S // ATT_SEQ_BLOCK),
        in_specs=[_token_spec(D_MODEL, ATT_SEQ_BLOCK), _const_spec((1, D_MODEL)),
                  _w_in_spec(2 * D_ATT, qk_block), _w_in_spec(2 * D_ATT, qk_block + 1),
                  _const_spec((1, ATT_HEAD_DIM)), _const_spec((1, ATT_HEAD_DIM)),
                  _const_spec((ATT_HEADS, BIAS_TABLE))],
        out_specs=_token_spec(D_ATT, ATT_SEQ_BLOCK),
        out_shape=jax.ShapeDtypeStruct((B, S, D_ATT), BF16),
        scratch_shapes=[pltpu.VMEM((ATT_SEQ_BLOCK, D_ATT), BF16),
                        pltpu.VMEM((D_ATT, PAST + ATT_SEQ_BLOCK), BF16),
                        pltpu.VMEM((PAST + ATT_SEQ_BLOCK, ATT_HEADS * V_SLOT), BF16),
                        pltpu.VMEM((ATT_SEQ_BLOCK, D_ATT), F32),
                        pltpu.VMEM((ATT_HEADS, Q_BLOCK, Q_KEYS), F32),
                        pltpu.VMEM((PROJ_BLOCK, D_MODEL), BF16)],
        compiler_params=_params(),
        name="att_branch",
    )(x, norm_g, w_in, w_in, q_norm_g, k_norm_g, bias_table)


def _bias_table(rel_bias):
    assert Q_KEYS - 1 - MAX_REL >= 0 and Q_KEYS - 1 - MAX_REL + 2 * MAX_REL + 1 >= BIAS_TABLE
    return jnp.pad(rel_bias[:, ::-1], ((0, 0), (Q_KEYS - 1 - MAX_REL, 0)), mode="edge")[:, :BIAS_TABLE]


def _mem_kernel(x_ref, ng_ref, w_ref, qg_ref, kt_ref, v_ref, o_ref):
    h = _rms_rows(x_ref[...], ng_ref[...]).astype(BF16)
    q_scale = MEM_HEAD_DIM ** -0.5 * LOG2E

    def head_slice(hh):
        return slice(hh * MEM_HEAD_DIM, (hh + 1) * MEM_HEAD_DIM)

    def q_head(hh):
        return (_rms_rows(_dot(h, w_ref[:, head_slice(hh)]), qg_ref[...]) * q_scale).astype(BF16)

    q = q_head(0)
    for hh in range(MEM_HEADS):
        hs = head_slice(hh)
        sg = jax.nn.silu(_dot(h, w_ref[:, D_MEM + hh * MEM_HEAD_DIM:D_MEM + (hh + 1) * MEM_HEAD_DIM]))
        q_next = q_head(hh + 1) if hh + 1 < MEM_HEADS else None
        s = _dot(q, kt_ref[hs, :])
        e = jnp.exp2(s - jnp.max(s, axis=-1, keepdims=True))
        l = jnp.sum(e, axis=-1, keepdims=True)
        o = _dot(e.astype(BF16), v_ref[:, hs]) / l
        o_ref[:, hs] = (o * sg).astype(BF16)
        q = q_next


def _mem_branch(x, norm_g, w_in, mem_q_norm_g, kt_mem, v_mem):
    B, S, _ = x.shape
    M = v_mem.shape[1]
    assert W_IN_MEM_COL % (2 * D_MEM) == 0
    kt_spec = pl.BlockSpec((None, D_MEM, M), lambda b, s: (b, 0, 0))
    v_spec = pl.BlockSpec((None, M, D_MEM), lambda b, s: (b, 0, 0))
    return pl.pallas_call(
        _mem_kernel,
        grid=(B, S // MEM_SEQ_BLOCK),
        in_specs=[_token_spec(D_MODEL, MEM_SEQ_BLOCK), _const_spec((1, D_MODEL)),
                  _w_in_spec(2 * D_MEM, W_IN_MEM_COL // (2 * D_MEM)), _const_spec((1, MEM_HEAD_DIM)),
                  kt_spec, v_spec],
        out_specs=_token_spec(D_MEM, MEM_SEQ_BLOCK),
        out_shape=jax.ShapeDtypeStruct((B, S, D_MEM), BF16),
        compiler_params=_params(),
        name="mem_branch",
    )(x, norm_g, w_in, mem_q_norm_g, kt_mem, v_mem)


def _rnn_out_kernel(x_ref, aa_ref, am_ref, ng_ref, p_ref, pt_ref, w_ref, cw_ref, cb_ref, wg_ref, bg_ref, lam_ref,
                    wmg0_ref, wmg1_ref, wmg2_ref, bm_ref, wpr_ref, wpa_ref, wpm_ref, wo_ref, o_ref,
                    xr_buf, a_buf, b_buf, sg_buf, gate_buf, g0_buf, zp_buf, carry_ref,
                    h_buf, hp_buf, y_buf, z_buf):
    R = RNN_ROWS
    chunks = [slice(c * COL_CHUNK, (c + 1) * COL_CHUNK) for c in range(D_MODEL // COL_CHUNK)]

    @pl.when(pl.program_id(1) == 0)
    def _():
        xr_buf[:, 0:CONV_HIST, :] = jnp.zeros((RNN_GROUPS, CONV_HIST, D_RNN), F32)
        carry_ref[...] = jnp.zeros(carry_ref.shape, F32)

    neg_log_a = LRU_C * jax.nn.softplus(-lam_ref[...])
    a_exp = -LOG2E * neg_log_a

    def merge_gate(n, cs):
        gs = slice(n * D_MODEL + cs.start, n * D_MODEL + cs.stop)
        return jax.nn.sigmoid(_dot(h_buf[...], (wmg0_ref, wmg1_ref, wmg2_ref)[n][:, cs]) + bm_ref[:, gs])

    for g in range(RNN_GROUPS):
        bs = slice(g * RNN_BATCH, (g + 1) * RNN_BATCH)
        xr_g = xr_buf.at[g]

        h_buf[...] = _rms_rows(x_ref[bs].reshape(R, D_MODEL), ng_ref[...]).astype(BF16)
        hp_buf[...] = _dot(p_ref[...], h_buf[...]).astype(BF16)
        for cs in chunks:
            xr_g[CONV_HIST:CONV_HIST + R, cs] = _dot(hp_buf[...], w_ref[:, cs])
            sg_buf[:, cs] = jax.nn.silu(_dot(hp_buf[...], w_ref[:, D_RNN + cs.start:D_RNN + cs.stop]))

        blocks_per_chunk = RNN_BLOCKS // len(chunks)
        for n in range(RNN_BLOCKS):
            cs = slice(n * RNN_BLOCK_DIM, (n + 1) * RNN_BLOCK_DIM)
            xc = cb_ref[:, cs]
            for j in range(CONV_WIDTH):
                xc = xc + cw_ref[j:j + 1, cs] * xr_g[j * RNN_BATCH:j * RNN_BATCH + R, cs]
            a_buf[:, cs] = xc
            gate_buf[:, 2 * n * RNN_BLOCK_DIM:2 * (n + 1) * RNN_BLOCK_DIM] = (
                _dot(xc.astype(BF16), wg_ref[n]) + bg_ref[n])
            if n % blocks_per_chunk == blocks_per_chunk - 1:
                cm = chunks[n // blocks_per_chunk]
                g0_buf[:, cm] = merge_gate(0, cm)
        xr_g[0:CONV_HIST, :] = xr_g[R:R + CONV_HIST, :]

        aa = aa_ref[bs].reshape(R, D_ATT)
        am = am_ref[bs].reshape(R, D_MEM)
        for cs in chunks:
            zp_buf[:, cs] = (merge_gate(1, cs) * _dot(aa, wpa_ref[:, cs])
                             + merge_gate(2, cs) * _dot(am, wpm_ref[:, cs]))

        for n in range(RNN_BLOCKS):
            cs = slice(n * RNN_BLOCK_DIM, (n + 1) * RNN_BLOCK_DIM)
            xc = a_buf[:, cs]
            r = jax.nn.sigmoid(gate_buf[:, 2 * n * RNN_BLOCK_DIM:(2 * n + 1) * RNN_BLOCK_DIM])
            i = jax.nn.sigmoid(gate_buf[:, (2 * n + 1) * RNN_BLOCK_DIM:(2 * n + 2) * RNN_BLOCK_DIM])
            a = jnp.exp2(r * a_exp[:, cs])
            one_minus_a2 = jnp.tanh(r * neg_log_a[:, cs]) * (1.0 + a * a)
            root = jnp.where(one_minus_a2 > 0.0, one_minus_a2 * lax.rsqrt(one_minus_a2), 0.0)
            a_buf[:, cs] = a
            b_buf[:, cs] = root * (i * xc)

        hc = carry_ref[g]
        for t in range(RNN_STEPS):
            rows = slice(t * RNN_BATCH, (t + 1) * RNN_BATCH)
            hc = a_buf[rows, :] * hc + b_buf[rows, :]
            b_buf[rows, :] = hc
        carry_ref[g] = hc

        for cs in chunks:
            y = (b_buf[:, cs] * sg_buf[:, cs]).astype(BF16)
            y_buf[:, cs] = _dot(pt_ref[...], y).astype(BF16)

        for cs in chunks:
            z_buf[:, cs] = (zp_buf[:, cs] + g0_buf[:, cs] * _dot(y_buf[...], wpr_ref[:, cs])).astype(BF16)
        for cs in chunks:
            out = x_ref[bs, :, cs].reshape(R, COL_CHUNK) + _dot(z_buf[...], wo_ref[:, cs])
            o_ref[bs, :, cs] = out.reshape(RNN_BATCH, RNN_STEPS, COL_CHUNK)


def _row_shuffle():
    out_row = np.arange(RNN_ROWS)
    t, b = out_row // RNN_BATCH, out_row % RNN_BATCH
    p = np.zeros((RNN_ROWS, RNN_ROWS), np.float32)
    p[out_row, b * RNN_STEPS + t] = 1.0
    return p


def _rnn_merge_out(x, a_att, a_mem, norm_g, w_in, conv_w, conv_b, w_gates, b_gates, lru_lambda,
                   b_merge, w_proj_rnn, w_proj_att, w_proj_mem, w_out):
    B, S, _ = x.shape
    p = _row_shuffle()
    assert W_IN_MERGE_COL % D_MODEL == 0
    merge_block = W_IN_MERGE_COL // D_MODEL
    blk = lambda width: pl.BlockSpec((RNN_GROUPS * RNN_BATCH, RNN_STEPS, width), lambda b, s: (b, s, 0))
    sq = _const_spec((D_MODEL, D_MODEL))
    plane = lambda dtype, width=D_RNN: pltpu.VMEM((RNN_ROWS, width), dtype)
    return pl.pallas_call(
        _rnn_out_kernel,
        grid=(B // (RNN_GROUPS * RNN_BATCH), S // RNN_STEPS),
        in_specs=[blk(D_MODEL), blk(D_ATT), blk(D_MEM), _const_spec((1, D_MODEL)),
                  _const_spec((RNN_ROWS, RNN_ROWS)), _const_spec((RNN_ROWS, RNN_ROWS)),
                  _w_in_spec(2 * D_RNN, 0),
                  _const_spec((CONV_WIDTH, D_RNN)), _const_spec((1, D_RNN)),
                  _const_spec((RNN_BLOCKS, RNN_BLOCK_DIM, 2 * RNN_BLOCK_DIM)),
                  _const_spec((RNN_BLOCKS, 1, 2 * RNN_BLOCK_DIM)), _const_spec((1, D_RNN)),
                  _w_in_spec(D_MODEL, merge_block), _w_in_spec(D_MODEL, merge_block + 1),
                  _w_in_spec(D_MODEL, merge_block + 2), _const_spec((1, N_BRANCHES * D_MODEL)),
                  sq, sq, sq, sq],
        out_specs=blk(D_MODEL),
        out_shape=jax.ShapeDtypeStruct((B, S, D_MODEL), F32),
        scratch_shapes=[pltpu.VMEM((RNN_GROUPS, CONV_HIST + RNN_ROWS, D_RNN), F32),
                        plane(F32), plane(F32), plane(F32), plane(F32, 2 * D_RNN), plane(F32), plane(F32),
                        pltpu.VMEM((RNN_GROUPS, RNN_BATCH, D_RNN), F32),
                        plane(BF16), plane(BF16), plane(BF16), plane(BF16)],
        compiler_params=_params(),
        name="rnn_merge_out",
    )(x, a_att, a_mem, norm_g, jnp.asarray(p, BF16), jnp.asarray(p.T, BF16), w_in, conv_w, conv_b, w_gates,
      b_gates, lru_lambda, w_in, w_in, w_in, b_merge, w_proj_rnn, w_proj_att, w_proj_mem, w_out)


@jax.jit
def kernel(x, mem, norm_g, mem_norm_g, w_in, b_merge, conv_w, conv_b, lru_wa, lru_ba, lru_wx, lru_bx, lru_lambda, q_norm_g, k_norm_g, rel_bias, w_mem_kv, mem_q_norm_g, mem_k_norm_g, w_proj_rnn, w_proj_att, w_proj_mem, w_out):
    B, S, D = x.shape
    assert D == D_MODEL and S % ATT_SEQ_BLOCK == 0 and S % MEM_SEQ_BLOCK == 0 and w_in.shape[0] == 1
    assert B % (RNN_GROUPS * RNN_BATCH) == 0

    def row(p):
        return p[0].reshape(1, -1)

    w = w_in[0].astype(BF16)
    w_gates = jnp.concatenate([lru_wa[0], lru_wx[0]], axis=-1).astype(BF16)
    b_gates = jnp.concatenate([lru_ba[0].reshape(RNN_BLOCKS, 1, RNN_BLOCK_DIM),
                               lru_bx[0].reshape(RNN_BLOCKS, 1, RNN_BLOCK_DIM)], axis=-1)
    ng = row(norm_g)

    kt_mem, v_mem = _mem_kv(mem, row(mem_norm_g), w_mem_kv[0].astype(BF16), row(mem_k_norm_g))
    a_att = _att_branch(x, ng, w, row(q_norm_g), row(k_norm_g), _bias_table(rel_bias[0]))
    a_mem = _mem_branch(x, ng, w, row(mem_q_norm_g), kt_mem, v_mem)
    return _rnn_merge_out(x, a_att, a_mem, ng, w, conv_w[0], row(conv_b), w_gates, b_gates, row(lru_lambda),
                          row(b_merge), w_proj_rnn[0].astype(BF16), w_proj_att[0].astype(BF16),
                          w_proj_mem[0].astype(BF16), w_out[0].astype(BF16))
```

```python
import jax
import jax.numpy as jnp
import numpy as np
from jax import lax
from jax.experimental import pallas as pl
from jax.experimental.pallas import tpu as pltpu

D_MODEL = 1024
CHUNK = 64
N_PAST_CHUNKS = 8
PAST = N_PAST_CHUNKS * CHUNK
BAND = PAST + CHUNK
D_RNN = 1024
RNN_BLOCKS = 8
RNN_BLOCK_DIM = D_RNN // RNN_BLOCKS
CONV_WIDTH = 4
LRU_C = 8.0
ATT_HEADS = 8
ATT_HEAD_DIM = 128
D_ATT = ATT_HEADS * ATT_HEAD_DIM
MAX_REL = 256
MEM_HEADS = 4
MEM_HEAD_DIM = 256
D_MEM = MEM_HEADS * MEM_HEAD_DIM
N_BRANCHES = 3
W_IN_ATT_COL = 2 * D_RNN
W_IN_MEM_COL = W_IN_ATT_COL + 4 * D_ATT
W_IN_MERGE_COL = W_IN_MEM_COL + 2 * D_MEM
EPS = 1e-6
NEG_INF = -1e30

V7X_SUBLANES = 8
V7X_MXU_DIM = 256
ATT_SEQ_BLOCK = 1024
MEM_SEQ_BLOCK = 1024
Q_BLOCK = 256
Q_KEYS = PAST + Q_BLOCK
COL_CHUNK = V7X_MXU_DIM
VMEM_LIMIT_BYTES = 56 * 1024 * 1024
RNN_BATCH = V7X_SUBLANES
RNN_STEPS = 32
RNN_ROWS = RNN_BATCH * RNN_STEPS
RNN_GROUPS = 2
CONV_HIST = (CONV_WIDTH - 1) * RNN_BATCH
BIAS_TABLE = 2 * PAST
V_SLOT = 2 * ATT_HEAD_DIM
LOG2E = float(np.log2(np.e))

BF16 = jnp.bfloat16
F32 = jnp.float32


def _dot(a, b):
    return jnp.dot(a, b, preferred_element_type=F32)


def _dot_nt(a, b):
    return lax.dot_general(a, b, (((1,), (1,)), ((), ())), preferred_element_type=F32)


def _rms_rows(xf, g):
    ms = jnp.mean(xf * xf, axis=-1, keepdims=True)
    return xf * lax.rsqrt(ms + EPS) * g


def _const_spec(shape):
    nd = len(shape)
    return pl.BlockSpec(shape, lambda *_: (0,) * nd, pipeline_mode=pl.Buffered(1))


def _w_in_spec(width, col_block):
    return pl.BlockSpec((D_MODEL, width), lambda *_: (0, col_block), pipeline_mode=pl.Buffered(1))


def _token_spec(width, rows):
    return pl.BlockSpec((None, rows, width), lambda b, s: (b, s, 0))


def _params():
    return pltpu.CompilerParams(
        dimension_semantics=("arbitrary", "arbitrary"),
        vmem_limit_bytes=VMEM_LIMIT_BYTES,
    )


def _mem_kv_kernel(mem_ref, g_ref, w_ref, kg_ref, k_ref, v_ref):
    mn = _rms_rows(mem_ref[...], g_ref[...]).astype(BF16)
    for h in range(MEM_HEADS):
        c0 = h * MEM_HEAD_DIM
        km = _dot(mn, w_ref[:, c0:c0 + MEM_HEAD_DIM])
        k_ref[c0:c0 + MEM_HEAD_DIM, :] = _rms_rows(km, kg_ref[...]).T.astype(BF16)
        vm = _dot(mn, w_ref[:, D_MEM + c0:D_MEM + c0 + MEM_HEAD_DIM])
        v_ref[:, c0:c0 + MEM_HEAD_DIM] = vm.astype(BF16)


def _mem_kv(mem, mem_norm_g, w_mem_kv, mem_k_norm_g):
    B, M, _ = mem.shape
    blk = pl.BlockSpec((None, M, D_MODEL), lambda b: (b, 0, 0))
    return pl.pallas_call(
        _mem_kv_kernel,
        grid=(B,),
        in_specs=[blk, _const_spec((1, D_MODEL)), _const_spec((D_MODEL, 2 * D_MEM)),
                  _const_spec((1, MEM_HEAD_DIM))],
        out_specs=[pl.BlockSpec((None, D_MEM, M), lambda b: (b, 0, 0)),
                   pl.BlockSpec((None, M, D_MEM), lambda b: (b, 0, 0))],
        out_shape=[jax.ShapeDtypeStruct((B, D_MEM, M), BF16), jax.ShapeDtypeStruct((B, M, D_MEM), BF16)],
        compiler_params=pltpu.CompilerParams(dimension_semantics=("arbitrary",),
                                             vmem_limit_bytes=VMEM_LIMIT_BYTES),
        name="mem_kv",
    )(mem, mem_norm_g, w_mem_kv, mem_k_norm_g)


def _att_kernel(x_ref, ng_ref, wqk_ref, wvg_ref, qg_ref, kg_ref, tbl_ref, o_ref,
                q_buf, kt_buf, v_buf, g_buf, bias_buf, h_buf):
    T = ATT_SEQ_BLOCK
    first = pl.program_id(1) == 0

    @pl.when(jnp.logical_and(first, pl.program_id(0) == 0))
    def _():
        i = lax.broadcasted_iota(jnp.int32, (Q_BLOCK, Q_KEYS), 0)
        j = lax.broadcasted_iota(jnp.int32, (Q_BLOCK, Q_KEYS), 1)
        off = j - (i // CHUNK) * CHUNK
        in_band = jnp.logical_and(off >= 0, off < BAND)
        for hh in range(ATT_HEADS):
            row = jnp.broadcast_to(tbl_ref[hh:hh + 1, :], (Q_BLOCK, BIAS_TABLE))
            win = pltpu.roll(row, BIAS_TABLE - (Q_BLOCK - 1), 1, stride=1, stride_axis=0)
            bias_buf[hh] = jnp.where(in_band, win[:, :Q_KEYS] * LOG2E, NEG_INF)
        v_buf[...] = jnp.ones(v_buf.shape, BF16)

    @pl.when(first)
    def _():
        kt_buf[:, 0:PAST] = jnp.zeros((D_ATT, PAST), BF16)
        for hh in range(ATT_HEADS):
            v_buf[0:PAST, hh * V_SLOT:hh * V_SLOT + ATT_HEAD_DIM] = jnp.zeros((PAST, ATT_HEAD_DIM), BF16)

    q_scale = ATT_HEAD_DIM ** -0.5 * LOG2E
    heads_per_chunk = COL_CHUNK // ATT_HEAD_DIM
    n_pieces = 2 * (D_ATT // COL_CHUNK)

    def project_piece(qb, piece):
        q0 = qb * Q_BLOCK
        rows = slice(q0, q0 + Q_BLOCK)
        new = slice(PAST + q0, PAST + q0 + Q_BLOCK)
        c, second_half = divmod(piece, 2)
        c0 = c * COL_CHUNK
        if piece == 0:
            h_buf[...] = _rms_rows(x_ref[rows, :], ng_ref[...]).astype(BF16)
        h = h_buf[...]
        if not second_half:
            q = _dot(h, wqk_ref[:, c0:c0 + COL_CHUNK])
            k = _dot(h, wqk_ref[:, D_ATT + c0:D_ATT + c0 + COL_CHUNK])
            for hh in range(heads_per_chunk):
                hs = slice(hh * ATT_HEAD_DIM, (hh + 1) * ATT_HEAD_DIM)
                os_ = slice(c0 + hh * ATT_HEAD_DIM, c0 + (hh + 1) * ATT_HEAD_DIM)
                q_buf[rows, os_] = (_rms_rows(q[:, hs], qg_ref[...]) * q_scale).astype(BF16)
                kt_buf[os_, new] = _rms_rows(k[:, hs], kg_ref[...]).T.astype(BF16)
        else:
            v = _dot(h, wvg_ref[:, c0:c0 + COL_CHUNK])
            for hh in range(heads_per_chunk):
                hs = slice(hh * ATT_HEAD_DIM, (hh + 1) * ATT_HEAD_DIM)
                head = c * heads_per_chunk + hh
                v_buf[new, head * V_SLOT:head * V_SLOT + ATT_HEAD_DIM] = v[:, hs].astype(BF16)
            g_buf[rows, c0:c0 + COL_CHUNK] = jax.nn.silu(
                _dot(h, wvg_ref[:, D_ATT + c0:D_ATT + c0 + COL_CHUNK]))

    col = lax.broadcasted_iota(jnp.int32, (Q_BLOCK, Q_KEYS), 1)

    def scores(qb, hh):
        q0 = qb * Q_BLOCK
        hs = slice(hh * ATT_HEAD_DIM, (hh + 1) * ATT_HEAD_DIM)
        s = _dot(q_buf[q0:q0 + Q_BLOCK, hs], kt_buf[hs, q0:q0 + Q_KEYS]) + bias_buf[hh]
        if q0 < PAST:
            s = jnp.where(col >= jnp.where(first, PAST - q0, 0), s, NEG_INF)
        return s

    def finish(qb, hh, s):
        q0 = qb * Q_BLOCK
        hs = slice(hh * ATT_HEAD_DIM, (hh + 1) * ATT_HEAD_DIM)
        e = jnp.exp2(s - jnp.max(s, axis=-1, keepdims=True))
        oe = _dot(e.astype(BF16), v_buf[q0:q0 + Q_KEYS, hh * V_SLOT:(hh + 1) * V_SLOT])
        o = oe[:, :ATT_HEAD_DIM] / oe[:, ATT_HEAD_DIM:]
        o_ref[q0:q0 + Q_BLOCK, hs] = (o * g_buf[q0:q0 + Q_BLOCK, hs]).astype(BF16)

    assert n_pieces == ATT_HEADS
    order = [(qb, hh) for qb in range(T // Q_BLOCK) for hh in range(ATT_HEADS)]
    for piece in range(n_pieces):
        project_piece(0, piece)
    s = scores(*order[0])
    for n, (qb, hh) in enumerate(order):
        if (qb + 1) * Q_BLOCK < T:
            project_piece(qb + 1, hh)
        s_next = scores(*order[n + 1]) if n + 1 < len(order) else None
        finish(qb, hh, s)
        s = s_next

    kt_buf[:, 0:PAST] = kt_buf[:, T:T + PAST]
    for hh in range(ATT_HEADS):
        vs = slice(hh * V_SLOT, hh * V_SLOT + ATT_HEAD_DIM)
        v_buf[0:PAST, vs] = v_buf[T:T + PAST, vs]


def _att_branch(x, norm_g, w_in, q_norm_g, k_norm_g, bias_table):
    B, S, _ = x.shape
    assert W_IN_ATT_COL % (2 * D_ATT) == 0
    qk_block = W_IN_ATT_COL // (2 * D_ATT)
    return pl.pallas_call(
        _att_kernel,
        grid=(B, S // ATT_SEQ_BLOCK),
        in_specs=[_token_spec(D_MODEL, ATT_SEQ_BLOCK), _const_spec((1, D_MODEL)),
                  _w_in_spec(2 * D_ATT, qk_block), _w_in_spec(2 * D_ATT, qk_block + 1),
                  _const_spec((1, ATT_HEAD_DIM)), _const_spec((1, ATT_HEAD_DIM)),
                  _const_spec((ATT_HEADS, BIAS_TABLE))],
        out_specs=_token_spec(D_ATT, ATT_SEQ_BLOCK),
        out_shape=jax.ShapeDtypeStruct((B, S, D_ATT), BF16),
        scratch_shapes=[pltpu.VMEM((ATT_SEQ_BLOCK, D_ATT), BF16),
                        pltpu.VMEM((D_ATT, PAST + ATT_SEQ_BLOCK), BF16),
                        pltpu.VMEM((PAST + ATT_SEQ_BLOCK, ATT_HEADS * V_SLOT), BF16),
                        pltpu.VMEM((ATT_SEQ_BLOCK, D_ATT), F32),
                        pltpu.VMEM((ATT_HEADS, Q_BLOCK, Q_KEYS), F32),
                        pltpu.VMEM((Q_BLOCK, D_MODEL), BF16)],
        compiler_params=_params(),
        name="att_branch",
    )(x, norm_g, w_in, w_in, q_norm_g, k_norm_g, bias_table)


def _bias_table(rel_bias):
    assert Q_KEYS - 1 - MAX_REL == BIAS_TABLE - (2 * MAX_REL + 1)
    return jnp.pad(rel_bias[:, ::-1], ((0, 0), (Q_KEYS - 1 - MAX_REL, 0)), mode="edge")


def _mem_kernel(x_ref, ng_ref, w_ref, qg_ref, kt_ref, v_ref, o_ref):
    h = _rms_rows(x_ref[...], ng_ref[...]).astype(BF16)
    q_scale = MEM_HEAD_DIM ** -0.5 * LOG2E

    def head_slice(hh):
        return slice(hh * MEM_HEAD_DIM, (hh + 1) * MEM_HEAD_DIM)

    def q_head(hh):
        return (_rms_rows(_dot(h, w_ref[:, head_slice(hh)]), qg_ref[...]) * q_scale).astype(BF16)

    q = q_head(0)
    for hh in range(MEM_HEADS):
        hs = head_slice(hh)
        sg = jax.nn.silu(_dot(h, w_ref[:, D_MEM + hh * MEM_HEAD_DIM:D_MEM + (hh + 1) * MEM_HEAD_DIM]))
        q_next = q_head(hh + 1) if hh + 1 < MEM_HEADS else None
        s = _dot(q, kt_ref[hs, :])
        e = jnp.exp2(s - jnp.max(s, axis=-1, keepdims=True))
        l = jnp.sum(e, axis=-1, keepdims=True)
        o = _dot(e.astype(BF16), v_ref[:, hs]) / l
        o_ref[:, hs] = (o * sg).astype(BF16)
        q = q_next


def _mem_branch(x, norm_g, w_in, mem_q_norm_g, kt_mem, v_mem):
    B, S, _ = x.shape
    M = v_mem.shape[1]
    assert W_IN_MEM_COL % (2 * D_MEM) == 0
    kt_spec = pl.BlockSpec((None, D_MEM, M), lambda b, s: (b, 0, 0))
    v_spec = pl.BlockSpec((None, M, D_MEM), lambda b, s: (b, 0, 0))
    return pl.pallas_call(
        _mem_kernel,
        grid=(B, S // MEM_SEQ_BLOCK),
        in_specs=[_token_spec(D_MODEL, MEM_SEQ_BLOCK), _const_spec((1, D_MODEL)),
                  _w_in_spec(2 * D_MEM, W_IN_MEM_COL // (2 * D_MEM)), _const_spec((1, MEM_HEAD_DIM)),
                  kt_spec, v_spec],
        out_specs=_token_spec(D_MEM, MEM_SEQ_BLOCK),
        out_shape=jax.ShapeDtypeStruct((B, S, D_MEM), BF16),
        compiler_params=_params(),
        name="mem_branch",
    )(x, norm_g, w_in, mem_q_norm_g, kt_mem, v_mem)


def _rnn_out_kernel(x_ref, aa_ref, am_ref, ng_ref, pt_ref, w_ref, cw_ref, cb_ref, wg_ref, bg_ref, lam_ref,
                    wmg0_ref, wmg1_ref, wmg2_ref, bm_ref, wpr_ref, wpa_ref, wpm_ref, wo_ref, o_ref,
                    xr_buf, a_buf, b_buf, sg_buf, gate_buf, g0_buf, zp_buf, carry_ref,
                    h_buf, hp_buf, y_buf, z_buf):
    R = RNN_ROWS
    chunks = [slice(c * COL_CHUNK, (c + 1) * COL_CHUNK) for c in range(D_MODEL // COL_CHUNK)]

    @pl.when(pl.program_id(1) == 0)
    def _():
        xr_buf[:, 0:CONV_HIST, :] = jnp.zeros((RNN_GROUPS, CONV_HIST, D_RNN), F32)
        carry_ref[...] = jnp.zeros(carry_ref.shape, F32)

    neg_log_a = LRU_C * jax.nn.softplus(-lam_ref[...])
    a_exp = -LOG2E * neg_log_a

    def merge_gate(n, cs):
        gs = slice(n * D_MODEL + cs.start, n * D_MODEL + cs.stop)
        return jax.nn.sigmoid(_dot(h_buf[...], (wmg0_ref, wmg1_ref, wmg2_ref)[n][:, cs]) + bm_ref[:, gs])

    for g in range(RNN_GROUPS):
        bs = slice(g * RNN_BATCH, (g + 1) * RNN_BATCH)
        xr_g = xr_buf.at[g]

        hn = _rms_rows(x_ref[bs].reshape(R, D_MODEL), ng_ref[...])
        h_buf[...] = hn.astype(BF16)
        hp_buf[...] = pltpu.einshape("btd->(tb)d", hn.reshape(RNN_BATCH, RNN_STEPS, D_MODEL)).astype(BF16)
        for cs in chunks:
            xr_g[CONV_HIST:CONV_HIST + R, cs] = _dot(hp_buf[...], w_ref[:, cs])
            sg_buf[:, cs] = jax.nn.silu(_dot(hp_buf[...], w_ref[:, D_RNN + cs.start:D_RNN + cs.stop]))

        blocks_per_chunk = RNN_BLOCKS // len(chunks)
        for n in range(RNN_BLOCKS):
            cs = slice(n * RNN_BLOCK_DIM, (n + 1) * RNN_BLOCK_DIM)
            xc = cb_ref[:, cs]
            for j in range(CONV_WIDTH):
                xc = xc + cw_ref[j:j + 1, cs] * xr_g[j * RNN_BATCH:j * RNN_BATCH + R, cs]
            a_buf[:, cs] = xc
            gate_buf[:, 2 * n * RNN_BLOCK_DIM:2 * (n + 1) * RNN_BLOCK_DIM] = (
                _dot(xc.astype(BF16), wg_ref[n]) + bg_ref[n])
            if n % blocks_per_chunk == blocks_per_chunk - 1:
                cm = chunks[n // blocks_per_chunk]
                g0_buf[:, cm] = merge_gate(0, cm)
        xr_g[0:CONV_HIST, :] = xr_g[R:R + CONV_HIST, :]

        aa = aa_ref[bs].reshape(R, D_ATT)
        am = am_ref[bs].reshape(R, D_MEM)
        for cs in chunks:
            zp_buf[:, cs] = (merge_gate(1, cs) * _dot(aa, wpa_ref[:, cs])
                             + merge_gate(2, cs) * _dot(am, wpm_ref[:, cs]))

        for n in range(RNN_BLOCKS):
            cs = slice(n * RNN_BLOCK_DIM, (n + 1) * RNN_BLOCK_DIM)
            xc = a_buf[:, cs]
            r = jax.nn.sigmoid(gate_buf[:, 2 * n * RNN_BLOCK_DIM:(2 * n + 1) * RNN_BLOCK_DIM])
            i = jax.nn.sigmoid(gate_buf[:, (2 * n + 1) * RNN_BLOCK_DIM:(2 * n + 2) * RNN_BLOCK_DIM])
            a = jnp.exp2(r * a_exp[:, cs])
            one_minus_a2 = jnp.tanh(r * neg_log_a[:, cs]) * (1.0 + a * a)
            root = jnp.where(one_minus_a2 > 0.0, one_minus_a2 * lax.rsqrt(one_minus_a2), 0.0)
            a_buf[:, cs] = a
            b_buf[:, cs] = root * (i * xc)

        hc = carry_ref[g]
        for t in range(RNN_STEPS):
            rows = slice(t * RNN_BATCH, (t + 1) * RNN_BATCH)
            hc = a_buf[rows, :] * hc + b_buf[rows, :]
            b_buf[rows, :] = hc
        carry_ref[g] = hc

        for cs in chunks:
            y = (b_buf[:, cs] * sg_buf[:, cs]).astype(BF16)
            y_buf[:, cs] = _dot(pt_ref[...], y).astype(BF16)

        for cs in chunks:
            z_buf[:, cs] = (zp_buf[:, cs] + g0_buf[:, cs] * _dot(y_buf[...], wpr_ref[:, cs])).astype(BF16)
        for cs in chunks:
            out = x_ref[bs, :, cs].reshape(R, COL_CHUNK) + _dot(z_buf[...], wo_ref[:, cs])
            o_ref[bs, :, cs] = out.reshape(RNN_BATCH, RNN_STEPS, COL_CHUNK)


def _row_shuffle():
    out_row = np.arange(RNN_ROWS)
    t, b = out_row // RNN_BATCH, out_row % RNN_BATCH
    p = np.zeros((RNN_ROWS, RNN_ROWS), np.float32)
    p[out_row, b * RNN_STEPS + t] = 1.0
    return p


def _rnn_merge_out(x, a_att, a_mem, norm_g, w_in, conv_w, conv_b, w_gates, b_gates, lru_lambda,
                   b_merge, w_proj_rnn, w_proj_att, w_proj_mem, w_out):
    B, S, _ = x.shape
    p = _row_shuffle()
    assert W_IN_MERGE_COL % D_MODEL == 0
    merge_block = W_IN_MERGE_COL // D_MODEL
    blk = lambda width: pl.BlockSpec((RNN_GROUPS * RNN_BATCH, RNN_STEPS, width), lambda b, s: (b, s, 0))
    sq = _const_spec((D_MODEL, D_MODEL))
    plane = lambda dtype, width=D_RNN: pltpu.VMEM((RNN_ROWS, width), dtype)
    return pl.pallas_call(
        _rnn_out_kernel,
        grid=(B // (RNN_GROUPS * RNN_BATCH), S // RNN_STEPS),
        in_specs=[blk(D_MODEL), blk(D_ATT), blk(D_MEM), _const_spec((1, D_MODEL)),
                  _const_spec((RNN_ROWS, RNN_ROWS)),
                  _w_in_spec(2 * D_RNN, 0),
                  _const_spec((CONV_WIDTH, D_RNN)), _const_spec((1, D_RNN)),
                  _const_spec((RNN_BLOCKS, RNN_BLOCK_DIM, 2 * RNN_BLOCK_DIM)),
                  _const_spec((RNN_BLOCKS, 1, 2 * RNN_BLOCK_DIM)), _const_spec((1, D_RNN)),
                  _w_in_spec(D_MODEL, merge_block), _w_in_spec(D_MODEL, merge_block + 1),
                  _w_in_spec(D_MODEL, merge_block + 2), _const_spec((1, N_BRANCHES * D_MODEL)),
                  sq, sq, sq, sq],
        out_specs=blk(D_MODEL),
        out_shape=jax.ShapeDtypeStruct((B, S, D_MODEL), F32),
        scratch_shapes=[pltpu.VMEM((RNN_GROUPS, CONV_HIST + RNN_ROWS, D_RNN), F32),
                        plane(F32), plane(F32), plane(F32), plane(F32, 2 * D_RNN), plane(F32), plane(F32),
                        pltpu.VMEM((RNN_GROUPS, RNN_BATCH, D_RNN), F32),
                        plane(BF16), plane(BF16), plane(BF16), plane(BF16)],
        compiler_params=_params(),
        name="rnn_merge_out",
    )(x, a_att, a_mem, norm_g, jnp.asarray(p.T, BF16), w_in, conv_w, conv_b, w_gates,
      b_gates, lru_lambda, w_in, w_in, w_in, b_merge, w_proj_rnn, w_proj_att, w_proj_mem, w_out)


@jax.jit
def kernel(x, mem, norm_g, mem_norm_g, w_in, b_merge, conv_w, conv_b, lru_wa, lru_ba, lru_wx, lru_bx, lru_lambda, q_norm_g, k_norm_g, rel_bias, w_mem_kv, mem_q_norm_g, mem_k_norm_g, w_proj_rnn, w_proj_att, w_proj_mem, w_out):
    B, S, D = x.shape
    assert D == D_MODEL and S % ATT_SEQ_BLOCK == 0 and S % MEM_SEQ_BLOCK == 0 and w_in.shape[0] == 1
    assert B % (RNN_GROUPS * RNN_BATCH) == 0

    def row(p):
        return p[0].reshape(1, -1)

    w = w_in[0].astype(BF16)
    w_gates = jnp.concatenate([lru_wa[0], lru_wx[0]], axis=-1).astype(BF16)
    b_gates = jnp.concatenate([lru_ba[0].reshape(RNN_BLOCKS, 1, RNN_BLOCK_DIM),
                               lru_bx[0].reshape(RNN_BLOCKS, 1, RNN_BLOCK_DIM)], axis=-1)
    ng = row(norm_g)

    kt_mem, v_mem = _mem_kv(mem, row(mem_norm_g), w_mem_kv[0].astype(BF16), row(mem_k_norm_g))
    a_att = _att_branch(x, ng, w, row(q_norm_g), row(k_norm_g), _bias_table(rel_bias[0]))
    a_mem = _mem_branch(x, ng, w, row(mem_q_norm_g), kt_mem, v_mem)
    return _rnn_merge_out(x, a_att, a_mem, ng, w, conv_w[0], row(conv_b), w_gates, b_gates, row(lru_lambda),
                          row(b_merge), w_proj_rnn[0].astype(BF16), w_proj_att[0].astype(BF16),
                          w_proj_mem[0].astype(BF16), w_out[0].astype(BF16))
```

```python
import jax
import jax.numpy as jnp
import numpy as np
from jax import lax
from jax.experimental import pallas as pl
from jax.experimental.pallas import tpu as pltpu

D_MODEL = 1024
CHUNK = 64
N_PAST_CHUNKS = 8
PAST = N_PAST_CHUNKS * CHUNK
BAND = PAST + CHUNK
D_RNN = 1024
RNN_BLOCKS = 8
RNN_BLOCK_DIM = D_RNN // RNN_BLOCKS
CONV_WIDTH = 4
LRU_C = 8.0
ATT_HEADS = 8
ATT_HEAD_DIM = 128
D_ATT = ATT_HEADS * ATT_HEAD_DIM
MAX_REL = 256
MEM_HEADS = 4
MEM_HEAD_DIM = 256
D_MEM = MEM_HEADS * MEM_HEAD_DIM
N_BRANCHES = 3
W_IN_ATT_COL = 2 * D_RNN
W_IN_MEM_COL = W_IN_ATT_COL + 4 * D_ATT
W_IN_MERGE_COL = W_IN_MEM_COL + 2 * D_MEM
EPS = 1e-6
NEG_INF = -1e30

V7X_SUBLANES = 8
V7X_MXU_DIM = 256
ATT_SEQ_BLOCK = 1024
MEM_SEQ_BLOCK = 1024
Q_BLOCK = 256
Q_KEYS = PAST + Q_BLOCK
COL_CHUNK = V7X_MXU_DIM
VMEM_LIMIT_BYTES = 56 * 1024 * 1024
RNN_BATCH = V7X_SUBLANES
RNN_STEPS = 32
RNN_ROWS = RNN_BATCH * RNN_STEPS
RNN_GROUPS = 2
CONV_HIST = (CONV_WIDTH - 1) * RNN_BATCH
BIAS_TABLE = 2 * PAST
V_SLOT = 2 * ATT_HEAD_DIM
LOG2E = float(np.log2(np.e))

BF16 = jnp.bfloat16
F32 = jnp.float32


def _dot(a, b):
    return jnp.dot(a, b, preferred_element_type=F32)


def _dot_nt(a, b):
    return lax.dot_general(a, b, (((1,), (1,)), ((), ())), preferred_element_type=F32)


def _rms_rows(xf, g):
    ms = jnp.mean(xf * xf, axis=-1, keepdims=True)
    return xf * lax.rsqrt(ms + EPS) * g


def _const_spec(shape):
    nd = len(shape)
    return pl.BlockSpec(shape, lambda *_: (0,) * nd, pipeline_mode=pl.Buffered(1))


def _w_in_spec(width, col_block):
    return pl.BlockSpec((D_MODEL, width), lambda *_: (0, col_block), pipeline_mode=pl.Buffered(1))


def _token_spec(width, rows):
    return pl.BlockSpec((None, rows, width), lambda b, s: (b, s, 0))


def _params():
    return pltpu.CompilerParams(
        dimension_semantics=("arbitrary", "arbitrary"),
        vmem_limit_bytes=VMEM_LIMIT_BYTES,
    )


def _mem_kv_kernel(mem_ref, g_ref, w_ref, kg_ref, k_ref, v_ref):
    mn = _rms_rows(mem_ref[...], g_ref[...]).astype(BF16)
    for h in range(MEM_HEADS):
        c0 = h * MEM_HEAD_DIM
        km = _dot(mn, w_ref[:, c0:c0 + MEM_HEAD_DIM])
        k_ref[c0:c0 + MEM_HEAD_DIM, :] = _rms_rows(km, kg_ref[...]).T.astype(BF16)
        vm = _dot(mn, w_ref[:, D_MEM + c0:D_MEM + c0 + MEM_HEAD_DIM])
        v_ref[:, c0:c0 + MEM_HEAD_DIM] = vm.astype(BF16)


def _mem_kv(mem, mem_norm_g, w_mem_kv, mem_k_norm_g):
    B, M, _ = mem.shape
    blk = pl.BlockSpec((None, M, D_MODEL), lambda b: (b, 0, 0))
    return pl.pallas_call(
        _mem_kv_kernel,
        grid=(B,),
        in_specs=[blk, _const_spec((1, D_MODEL)), _const_spec((D_MODEL, 2 * D_MEM)),
                  _const_spec((1, MEM_HEAD_DIM))],
        out_specs=[pl.BlockSpec((None, D_MEM, M), lambda b: (b, 0, 0)),
                   pl.BlockSpec((None, M, D_MEM), lambda b: (b, 0, 0))],
        out_shape=[jax.ShapeDtypeStruct((B, D_MEM, M), BF16), jax.ShapeDtypeStruct((B, M, D_MEM), BF16)],
        compiler_params=pltpu.CompilerParams(dimension_semantics=("arbitrary",),
                                             vmem_limit_bytes=VMEM_LIMIT_BYTES),
        name="mem_kv",
    )(mem, mem_norm_g, w_mem_kv, mem_k_norm_g)


def _att_kernel(x_ref, ng_ref, wqk_ref, wvg_ref, qg_ref, kg_ref, tbl_ref, o_ref,
                q_buf, kt_buf, v_buf, g_buf, bias_buf, h_buf):
    T = ATT_SEQ_BLOCK
    first = pl.program_id(1) == 0

    @pl.when(jnp.logical_and(first, pl.program_id(0) == 0))
    def _():
        i = lax.broadcasted_iota(jnp.int32, (Q_BLOCK, Q_KEYS), 0)
        j = lax.broadcasted_iota(jnp.int32, (Q_BLOCK, Q_KEYS), 1)
        off = j - (i // CHUNK) * CHUNK
        in_band = jnp.logical_and(off >= 0, off < BAND)
        for hh in range(ATT_HEADS):
            row = jnp.broadcast_to(tbl_ref[hh:hh + 1, :], (Q_BLOCK, BIAS_TABLE))
            win = pltpu.roll(row, BIAS_TABLE - (Q_BLOCK - 1), 1, stride=1, stride_axis=0)
            bias_buf[hh] = jnp.where(in_band, win[:, :Q_KEYS] * LOG2E, NEG_INF)
        v_buf[...] = jnp.ones(v_buf.shape, BF16)

    @pl.when(first)
    def _():
        kt_buf[:, 0:PAST] = jnp.zeros((D_ATT, PAST), BF16)
        for hh in range(ATT_HEADS):
            v_buf[0:PAST, hh * V_SLOT:hh * V_SLOT + ATT_HEAD_DIM] = jnp.zeros((PAST, ATT_HEAD_DIM), BF16)

    q_scale = ATT_HEAD_DIM ** -0.5 * LOG2E
    heads_per_chunk = COL_CHUNK // ATT_HEAD_DIM
    n_pieces = 2 * (D_ATT // COL_CHUNK)

    def project_piece(qb, piece):
        q0 = qb * Q_BLOCK
        rows = slice(q0, q0 + Q_BLOCK)
        new = slice(PAST + q0, PAST + q0 + Q_BLOCK)
        c, second_half = divmod(piece, 2)
        c0 = c * COL_CHUNK
        if piece == 0:
            h_buf[...] = _rms_rows(x_ref[rows, :], ng_ref[...]).astype(BF16)
        h = h_buf[...]
        if not second_half:
            q = _dot(h, wqk_ref[:, c0:c0 + COL_CHUNK])
            k = _dot(h, wqk_ref[:, D_ATT + c0:D_ATT + c0 + COL_CHUNK])
            for hh in range(heads_per_chunk):
                hs = slice(hh * ATT_HEAD_DIM, (hh + 1) * ATT_HEAD_DIM)
                os_ = slice(c0 + hh * ATT_HEAD_DIM, c0 + (hh + 1) * ATT_HEAD_DIM)
                q_buf[rows, os_] = (_rms_rows(q[:, hs], qg_ref[...]) * q_scale).astype(BF16)
                kt_buf[os_, new] = _rms_rows(k[:, hs], kg_ref[...]).T.astype(BF16)
        else:
            v = _dot(h, wvg_ref[:, c0:c0 + COL_CHUNK])
            for hh in range(heads_per_chunk):
                hs = slice(hh * ATT_HEAD_DIM, (hh + 1) * ATT_HEAD_DIM)
                head = c * heads_per_chunk + hh
                v_buf[new, head * V_SLOT:head * V_SLOT + ATT_HEAD_DIM] = v[:, hs].astype(BF16)
            g_buf[rows, c0:c0 + COL_CHUNK] = jax.nn.silu(
                _dot(h, wvg_ref[:, D_ATT + c0:D_ATT + c0 + COL_CHUNK]))

    col = lax.broadcasted_iota(jnp.int32, (Q_BLOCK, Q_KEYS), 1)

    def scores(qb, hh):
        q0 = qb * Q_BLOCK
        hs = slice(hh * ATT_HEAD_DIM, (hh + 1) * ATT_HEAD_DIM)
        s = _dot(q_buf[q0:q0 + Q_BLOCK, hs], kt_buf[hs, q0:q0 + Q_KEYS]) + bias_buf[hh]
        if q0 < PAST:
            s = jnp.where(col >= jnp.where(first, PAST - q0, 0), s, NEG_INF)
        return s

    def finish(qb, hh, s):
        q0 = qb * Q_BLOCK
        hs = slice(hh * ATT_HEAD_DIM, (hh + 1) * ATT_HEAD_DIM)
        e = jnp.exp2(s - jnp.max(s, axis=-1, keepdims=True))
        oe = _dot(e.astype(BF16), v_buf[q0:q0 + Q_KEYS, hh * V_SLOT:(hh + 1) * V_SLOT])
        o = oe[:, :ATT_HEAD_DIM] / oe[:, ATT_HEAD_DIM:]
        o_ref[q0:q0 + Q_BLOCK, hs] = (o * g_buf[q0:q0 + Q_BLOCK, hs]).astype(BF16)

    assert n_pieces == ATT_HEADS
    order = [(qb, hh) for qb in range(T // Q_BLOCK) for hh in range(ATT_HEADS)]
    for piece in range(n_pieces):
        project_piece(0, piece)
    s = scores(*order[0])
    for n, (qb, hh) in enumerate(order):
        if (qb + 1) * Q_BLOCK < T:
            project_piece(qb + 1, hh)
        s_next = scores(*order[n + 1]) if n + 1 < len(order) else None
        finish(qb, hh, s)
        s = s_next

    kt_buf[:, 0:PAST] = kt_buf[:, T:T + PAST]
    for hh in range(ATT_HEADS):
        vs = slice(hh * V_SLOT, hh * V_SLOT + ATT_HEAD_DIM)
        v_buf[0:PAST, vs] = v_buf[T:T + PAST, vs]


def _att_branch(x, norm_g, w_in, q_norm_g, k_norm_g, bias_table):
    B, S, _ = x.shape
    assert W_IN_ATT_COL % (2 * D_ATT) == 0
    qk_block = W_IN_ATT_COL // (2 * D_ATT)
    return pl.pallas_call(
        _att_kernel,
        grid=(B, S // ATT_SEQ_BLOCK),
        in_specs=[_token_spec(D_MODEL, ATT_SEQ_BLOCK), _const_spec((1, D_MODEL)),
                  _w_in_spec(2 * D_ATT, qk_block), _w_in_spec(2 * D_ATT, qk_block + 1),
                  _const_spec((1, ATT_HEAD_DIM)), _const_spec((1, ATT_HEAD_DIM)),
                  _const_spec((ATT_HEADS, BIAS_TABLE))],
        out_specs=_token_spec(D_ATT, ATT_SEQ_BLOCK),
        out_shape=jax.ShapeDtypeStruct((B, S, D_ATT), BF16),
        scratch_shapes=[pltpu.VMEM((ATT_SEQ_BLOCK, D_ATT), BF16),
                        pltpu.VMEM((D_ATT, PAST + ATT_SEQ_BLOCK), BF16),
                        pltpu.VMEM((PAST + ATT_SEQ_BLOCK, ATT_HEADS * V_SLOT), BF16),
                        pltpu.VMEM((ATT_SEQ_BLOCK, D_ATT), F32),
                        pltpu.VMEM((ATT_HEADS, Q_BLOCK, Q_KEYS), F32),
                        pltpu.VMEM((Q_BLOCK, D_MODEL), BF16)],
        compiler_params=_params(),
        name="att_branch",
    )(x, norm_g, w_in, w_in, q_norm_g, k_norm_g, bias_table)


def _bias_table(rel_bias):
    assert Q_KEYS - 1 - MAX_REL == BIAS_TABLE - (2 * MAX_REL + 1)
    return jnp.pad(rel_bias[:, ::-1], ((0, 0), (Q_KEYS - 1 - MAX_REL, 0)), mode="edge")


def _mem_kernel(x_ref, ng_ref, w_ref, qg_ref, kt_ref, v_ref, o_ref):
    h = _rms_rows(x_ref[...], ng_ref[...]).astype(BF16)
    q_scale = MEM_HEAD_DIM ** -0.5 * LOG2E

    def head_slice(hh):
        return slice(hh * MEM_HEAD_DIM, (hh + 1) * MEM_HEAD_DIM)

    def q_head(hh):
        return (_rms_rows(_dot(h, w_ref[:, head_slice(hh)]), qg_ref[...]) * q_scale).astype(BF16)

    q = q_head(0)
    for hh in range(MEM_HEADS):
        hs = head_slice(hh)
        sg = jax.nn.silu(_dot(h, w_ref[:, D_MEM + hh * MEM_HEAD_DIM:D_MEM + (hh + 1) * MEM_HEAD_DIM]))
        q_next = q_head(hh + 1) if hh + 1 < MEM_HEADS else None
        s = _dot(q, kt_ref[hs, :])
        e = jnp.exp2(s - jnp.max(s, axis=-1, keepdims=True))
        l = jnp.sum(e, axis=-1, keepdims=True)
        o = _dot(e.astype(BF16), v_ref[:, hs]) / l
        o_ref[:, hs] = (o * sg).astype(BF16)
        q = q_next


def _mem_branch(x, norm_g, w_in, mem_q_norm_g, kt_mem, v_mem):
    B, S, _ = x.shape
    M = v_mem.shape[1]
    assert W_IN_MEM_COL % (2 * D_MEM) == 0
    kt_spec = pl.BlockSpec((None, D_MEM, M), lambda b, s: (b, 0, 0))
    v_spec = pl.BlockSpec((None, M, D_MEM), lambda b, s: (b, 0, 0))
    return pl.pallas_call(
        _mem_kernel,
        grid=(B, S // MEM_SEQ_BLOCK),
        in_specs=[_token_spec(D_MODEL, MEM_SEQ_BLOCK), _const_spec((1, D_MODEL)),
                  _w_in_spec(2 * D_MEM, W_IN_MEM_COL // (2 * D_MEM)), _const_spec((1, MEM_HEAD_DIM)),
                  kt_spec, v_spec],
        out_specs=_token_spec(D_MEM, MEM_SEQ_BLOCK),
        out_shape=jax.ShapeDtypeStruct((B, S, D_MEM), BF16),
        compiler_params=_params(),
        name="mem_branch",
    )(x, norm_g, w_in, mem_q_norm_g, kt_mem, v_mem)


def _rnn_out_kernel(x_ref, aa_ref, am_ref, ng_ref, p_ref, pt_ref, w_ref, cw_ref, cb_ref, wg_ref, bg_ref, lam_ref,
                    wmg0_ref, wmg1_ref, wmg2_ref, bm_ref, wpr_ref, wpa_ref, wpm_ref, wo_ref, o_ref,
                    xr_buf, a_buf, b_buf, sg_buf, gate_buf, g0_buf, zp_buf, carry_ref,
                    h_buf, hp_buf, y_buf, z_buf):
    R = RNN_ROWS
    chunks = [slice(c * COL_CHUNK, (c + 1) * COL_CHUNK) for c in range(D_MODEL // COL_CHUNK)]

    @pl.when(pl.program_id(1) == 0)
    def _():
        xr_buf[:, 0:CONV_HIST, :] = jnp.zeros((RNN_GROUPS, CONV_HIST, D_RNN), F32)
        carry_ref[...] = jnp.zeros(carry_ref.shape, F32)

    neg_log_a = LRU_C * jax.nn.softplus(-lam_ref[...])
    a_exp = -LOG2E * neg_log_a

    def merge_gate(n, cs):
        gs = slice(n * D_MODEL + cs.start, n * D_MODEL + cs.stop)
        return jax.nn.sigmoid(_dot(h_buf[...], (wmg0_ref, wmg1_ref, wmg2_ref)[n][:, cs]) + bm_ref[:, gs])

    def group(g, carry):
        bs = pl.ds(g * RNN_BATCH, RNN_BATCH)
        xr_g = xr_buf.at[g]

        h_buf[...] = _rms_rows(x_ref[bs].reshape(R, D_MODEL), ng_ref[...]).astype(BF16)
        hp_buf[...] = _dot(p_ref[...], h_buf[...]).astype(BF16)
        for cs in chunks:
            xr_g[CONV_HIST:CONV_HIST + R, cs] = _dot(hp_buf[...], w_ref[:, cs])
            sg_buf[:, cs] = jax.nn.silu(_dot(hp_buf[...], w_ref[:, D_RNN + cs.start:D_RNN + cs.stop]))

        blocks_per_chunk = RNN_BLOCKS // len(chunks)
        for n in range(RNN_BLOCKS):
            cs = slice(n * RNN_BLOCK_DIM, (n + 1) * RNN_BLOCK_DIM)
            xc = cb_ref[:, cs]
            for j in range(CONV_WIDTH):
                xc = xc + cw_ref[j:j + 1, cs] * xr_g[j * RNN_BATCH:j * RNN_BATCH + R, cs]
            a_buf[:, cs] = xc
            gate_buf[:, 2 * n * RNN_BLOCK_DIM:2 * (n + 1) * RNN_BLOCK_DIM] = (
                _dot(xc.astype(BF16), wg_ref[n]) + bg_ref[n])
            if n % blocks_per_chunk == blocks_per_chunk - 1:
                cm = chunks[n // blocks_per_chunk]
                g0_buf[:, cm] = merge_gate(0, cm)
        xr_g[0:CONV_HIST, :] = xr_g[R:R + CONV_HIST, :]

        aa = aa_ref[bs].reshape(R, D_ATT)
        am = am_ref[bs].reshape(R, D_MEM)
        for cs in chunks:
            zp_buf[:, cs] = (merge_gate(1, cs) * _dot(aa, wpa_ref[:, cs])
                             + merge_gate(2, cs) * _dot(am, wpm_ref[:, cs]))

        for n in range(RNN_BLOCKS):
            cs = slice(n * RNN_BLOCK_DIM, (n + 1) * RNN_BLOCK_DIM)
            xc = a_buf[:, cs]
            r = jax.nn.sigmoid(gate_buf[:, 2 * n * RNN_BLOCK_DIM:(2 * n + 1) * RNN_BLOCK_DIM])
            i = jax.nn.sigmoid(gate_buf[:, (2 * n + 1) * RNN_BLOCK_DIM:(2 * n + 2) * RNN_BLOCK_DIM])
            a = jnp.exp2(r * a_exp[:, cs])
            one_minus_a2 = jnp.tanh(r * neg_log_a[:, cs]) * (1.0 + a * a)
            root = jnp.where(one_minus_a2 > 0.0, one_minus_a2 * lax.rsqrt(one_minus_a2), 0.0)
            a_buf[:, cs] = a
            b_buf[:, cs] = root * (i * xc)

        hc = carry_ref[g]
        for t in range(RNN_STEPS):
            rows = slice(t * RNN_BATCH, (t + 1) * RNN_BATCH)
            hc = a_buf[rows, :] * hc + b_buf[rows, :]
            b_buf[rows, :] = hc
        carry_ref[g] = hc

        for cs in chunks:
            y = (b_buf[:, cs] * sg_buf[:, cs]).astype(BF16)
            y_buf[:, cs] = _dot(pt_ref[...], y).astype(BF16)

        for cs in chunks:
            z_buf[:, cs] = (zp_buf[:, cs] + g0_buf[:, cs] * _dot(y_buf[...], wpr_ref[:, cs])).astype(BF16)
        for cs in chunks:
            out = x_ref[bs, :, cs].reshape(R, COL_CHUNK) + _dot(z_buf[...], wo_ref[:, cs])
            o_ref[bs, :, cs] = out.reshape(RNN_BATCH, RNN_STEPS, COL_CHUNK)
        return carry

    lax.fori_loop(0, RNN_GROUPS, group, 0)


def _row_shuffle():
    out_row = np.arange(RNN_ROWS)
    t, b = out_row // RNN_BATCH, out_row % RNN_BATCH
    p = np.zeros((RNN_ROWS, RNN_ROWS), np.float32)
    p[out_row, b * RNN_STEPS + t] = 1.0
    return p


def _rnn_merge_out(x, a_att, a_mem, norm_g, w_in, conv_w, conv_b, w_gates, b_gates, lru_lambda,
                   b_merge, w_proj_rnn, w_proj_att, w_proj_mem, w_out):
    B, S, _ = x.shape
    p = _row_shuffle()
    assert W_IN_MERGE_COL % D_MODEL == 0
    merge_block = W_IN_MERGE_COL // D_MODEL
    blk = lambda width: pl.BlockSpec((RNN_GROUPS * RNN_BATCH, RNN_STEPS, width), lambda b, s: (b, s, 0))
    sq = _const_spec((D_MODEL, D_MODEL))
    plane = lambda dtype, width=D_RNN: pltpu.VMEM((RNN_ROWS, width), dtype)
    return pl.pallas_call(
        _rnn_out_kernel,
        grid=(B // (RNN_GROUPS * RNN_BATCH), S // RNN_STEPS),
        in_specs=[blk(D_MODEL), blk(D_ATT), blk(D_MEM), _const_spec((1, D_MODEL)),
                  _const_spec((RNN_ROWS, RNN_ROWS)), _const_spec((RNN_ROWS, RNN_ROWS)),
                  _w_in_spec(2 * D_RNN, 0),
                  _const_spec((CONV_WIDTH, D_RNN)), _const_spec((1, D_RNN)),
                  _const_spec((RNN_BLOCKS, RNN_BLOCK_DIM, 2 * RNN_BLOCK_DIM)),
                  _const_spec((RNN_BLOCKS, 1, 2 * RNN_BLOCK_DIM)), _const_spec((1, D_RNN)),
                  _w_in_spec(D_MODEL, merge_block), _w_in_spec(D_MODEL, merge_block + 1),
                  _w_in_spec(D_MODEL, merge_block + 2), _const_spec((1, N_BRANCHES * D_MODEL)),
                  sq, sq, sq, sq],
        out_specs=blk(D_MODEL),
        out_shape=jax.ShapeDtypeStruct((B, S, D_MODEL), F32),
        scratch_shapes=[pltpu.VMEM((RNN_GROUPS, CONV_HIST + RNN_ROWS, D_RNN), F32),
                        plane(F32), plane(F32), plane(F32), plane(F32, 2 * D_RNN), plane(F32), plane(F32),
                        pltpu.VMEM((RNN_GROUPS, RNN_BATCH, D_RNN), F32),
                        plane(BF16), plane(BF16), plane(BF16), plane(BF16)],
        compiler_params=_params(),
        name="rnn_merge_out",
    )(x, a_att, a_mem, norm_g, jnp.asarray(p, BF16), jnp.asarray(p.T, BF16), w_in, conv_w, conv_b, w_gates,
      b_gates, lru_lambda, w_in, w_in, w_in, b_merge, w_proj_rnn, w_proj_att, w_proj_mem, w_out)


@jax.jit
def kernel(x, mem, norm_g, mem_norm_g, w_in, b_merge, conv_w, conv_b, lru_wa, lru_ba, lru_wx, lru_bx, lru_lambda, q_norm_g, k_norm_g, rel_bias, w_mem_kv, mem_q_norm_g, mem_k_norm_g, w_proj_rnn, w_proj_att, w_proj_mem, w_out):
    B, S, D = x.shape
    assert D == D_MODEL and S % ATT_SEQ_BLOCK == 0 and S % MEM_SEQ_BLOCK == 0 and w_in.shape[0] == 1
    assert B % (RNN_GROUPS * RNN_BATCH) == 0

    def row(p):
        return p[0].reshape(1, -1)

    w = w_in[0].astype(BF16)
    w_gates = jnp.concatenate([lru_wa[0], lru_wx[0]], axis=-1).astype(BF16)
    b_gates = jnp.concatenate([lru_ba[0].reshape(RNN_BLOCKS, 1, RNN_BLOCK_DIM),
                               lru_bx[0].reshape(RNN_BLOCKS, 1, RNN_BLOCK_DIM)], axis=-1)
    ng = row(norm_g)

    kt_mem, v_mem = _mem_kv(mem, row(mem_norm_g), w_mem_kv[0].astype(BF16), row(mem_k_norm_g))
    a_att = _att_branch(x, ng, w, row(q_norm_g), row(k_norm_g), _bias_table(rel_bias[0]))
    a_mem = _mem_branch(x, ng, w, row(mem_q_norm_g), kt_mem, v_mem)
    return _rnn_merge_out(x, a_att, a_mem, ng, w, conv_w[0], row(conv_b), w_gates, b_gates, row(lru_lambda),
                          row(b_merge), w_proj_rnn[0].astype(BF16), w_proj_att[0].astype(BF16),
                          w_proj_mem[0].astype(BF16), w_out[0].astype(BF16))
```

```python
import jax
import jax.numpy as jnp
import numpy as np
from jax import lax
from jax.experimental import pallas as pl
from jax.experimental.pallas import tpu as pltpu

D_MODEL = 1024
CHUNK = 64
N_PAST_CHUNKS = 8
PAST = N_PAST_CHUNKS * CHUNK
BAND = PAST + CHUNK
D_RNN = 1024
RNN_BLOCKS = 8
RNN_BLOCK_DIM = D_RNN // RNN_BLOCKS
CONV_WIDTH = 4
LRU_C = 8.0
ATT_HEADS = 8
ATT_HEAD_DIM = 128
D_ATT = ATT_HEADS * ATT_HEAD_DIM
MAX_REL = 256
MEM_HEADS = 4
MEM_HEAD_DIM = 256
D_MEM = MEM_HEADS * MEM_HEAD_DIM
N_BRANCHES = 3
W_IN_ATT_COL = 2 * D_RNN
W_IN_MEM_COL = W_IN_ATT_COL + 4 * D_ATT
W_IN_MERGE_COL = W_IN_MEM_COL + 2 * D_MEM
EPS = 1e-6
NEG_INF = -1e30

V7X_SUBLANES = 8
V7X_MXU_DIM = 256
ATT_SEQ_BLOCK = 1024
MEM_SEQ_BLOCK = 1024
Q_BLOCK = 256
Q_KEYS = PAST + Q_BLOCK
COL_CHUNK = V7X_MXU_DIM
VMEM_LIMIT_BYTES = 56 * 1024 * 1024
RNN_BATCH = V7X_SUBLANES
RNN_STEPS = 32
RNN_ROWS = RNN_BATCH * RNN_STEPS
RNN_GROUPS = 2
CONV_HIST = (CONV_WIDTH - 1) * RNN_BATCH
BIAS_TABLE = 2 * PAST
V_SLOT = 2 * ATT_HEAD_DIM
LOG2E = float(np.log2(np.e))

BF16 = jnp.bfloat16
F32 = jnp.float32


def _dot(a, b):
    return jnp.dot(a, b, preferred_element_type=F32)


def _dot_nt(a, b):
    return lax.dot_general(a, b, (((1,), (1,)), ((), ())), preferred_element_type=F32)


def _rms_rows(xf, g):
    ms = jnp.mean(xf * xf, axis=-1, keepdims=True)
    return xf * lax.rsqrt(ms + EPS) * g


def _const_spec(shape):
    nd = len(shape)
    return pl.BlockSpec(shape, lambda *_: (0,) * nd, pipeline_mode=pl.Buffered(1))


def _w_in_spec(width, col_block):
    return pl.BlockSpec((D_MODEL, width), lambda *_: (0, col_block), pipeline_mode=pl.Buffered(1))


def _token_spec(width, rows):
    return pl.BlockSpec((None, rows, width), lambda b, s: (b, s, 0))


def _params():
    return pltpu.CompilerParams(
        dimension_semantics=("arbitrary", "arbitrary"),
        vmem_limit_bytes=VMEM_LIMIT_BYTES,
    )


def _att_kernel(x_ref, ng_ref, wqk_ref, wvg_ref, qg_ref, kg_ref, tbl_ref, o_ref,
                q_buf, kt_buf, v_buf, g_buf, bias_buf, h_buf):
    T = ATT_SEQ_BLOCK
    first = pl.program_id(1) == 0

    @pl.when(jnp.logical_and(first, pl.program_id(0) == 0))
    def _():
        i = lax.broadcasted_iota(jnp.int32, (Q_BLOCK, Q_KEYS), 0)
        j = lax.broadcasted_iota(jnp.int32, (Q_BLOCK, Q_KEYS), 1)
        off = j - (i // CHUNK) * CHUNK
        in_band = jnp.logical_and(off >= 0, off < BAND)
        for hh in range(ATT_HEADS):
            row = jnp.broadcast_to(tbl_ref[hh:hh + 1, :], (Q_BLOCK, BIAS_TABLE))
            win = pltpu.roll(row, BIAS_TABLE - (Q_BLOCK - 1), 1, stride=1, stride_axis=0)
            bias_buf[hh] = jnp.where(in_band, win[:, :Q_KEYS] * LOG2E, NEG_INF)
        v_buf[...] = jnp.ones(v_buf.shape, BF16)

    @pl.when(first)
    def _():
        kt_buf[:, 0:PAST] = jnp.zeros((D_ATT, PAST), BF16)
        for hh in range(ATT_HEADS):
            v_buf[0:PAST, hh * V_SLOT:hh * V_SLOT + ATT_HEAD_DIM] = jnp.zeros((PAST, ATT_HEAD_DIM), BF16)

    q_scale = ATT_HEAD_DIM ** -0.5 * LOG2E
    heads_per_chunk = COL_CHUNK // ATT_HEAD_DIM
    n_pieces = 2 * (D_ATT // COL_CHUNK)

    def project_piece(qb, piece):
        q0 = qb * Q_BLOCK
        rows = slice(q0, q0 + Q_BLOCK)
        new = slice(PAST + q0, PAST + q0 + Q_BLOCK)
        c, second_half = divmod(piece, 2)
        c0 = c * COL_CHUNK
        if piece == 0:
            h_buf[...] = _rms_rows(x_ref[rows, :], ng_ref[...]).astype(BF16)
        h = h_buf[...]
        if not second_half:
            q = _dot(h, wqk_ref[:, c0:c0 + COL_CHUNK])
            k = _dot(h, wqk_ref[:, D_ATT + c0:D_ATT + c0 + COL_CHUNK])
            for hh in range(heads_per_chunk):
                hs = slice(hh * ATT_HEAD_DIM, (hh + 1) * ATT_HEAD_DIM)
                os_ = slice(c0 + hh * ATT_HEAD_DIM, c0 + (hh + 1) * ATT_HEAD_DIM)
                q_buf[rows, os_] = (_rms_rows(q[:, hs], qg_ref[...]) * q_scale).astype(BF16)
                kt_buf[os_, new] = _rms_rows(k[:, hs], kg_ref[...]).T.astype(BF16)
        else:
            v = _dot(h, wvg_ref[:, c0:c0 + COL_CHUNK])
            for hh in range(heads_per_chunk):
                hs = slice(hh * ATT_HEAD_DIM, (hh + 1) * ATT_HEAD_DIM)
                head = c * heads_per_chunk + hh
                v_buf[new, head * V_SLOT:head * V_SLOT + ATT_HEAD_DIM] = v[:, hs].astype(BF16)
            g_buf[rows, c0:c0 + COL_CHUNK] = jax.nn.silu(
                _dot(h, wvg_ref[:, D_ATT + c0:D_ATT + c0 + COL_CHUNK]))

    col = lax.broadcasted_iota(jnp.int32, (Q_BLOCK, Q_KEYS), 1)

    def scores(qb, hh):
        q0 = qb * Q_BLOCK
        hs = slice(hh * ATT_HEAD_DIM, (hh + 1) * ATT_HEAD_DIM)
        s = _dot(q_buf[q0:q0 + Q_BLOCK, hs], kt_buf[hs, q0:q0 + Q_KEYS]) + bias_buf[hh]
        if q0 < PAST:
            s = jnp.where(col >= jnp.where(first, PAST - q0, 0), s, NEG_INF)
        return s

    def finish(qb, hh, s):
        q0 = qb * Q_BLOCK
        hs = slice(hh * ATT_HEAD_DIM, (hh + 1) * ATT_HEAD_DIM)
        e = jnp.exp2(s - jnp.max(s, axis=-1, keepdims=True))
        oe = _dot(e.astype(BF16), v_buf[q0:q0 + Q_KEYS, hh * V_SLOT:(hh + 1) * V_SLOT])
        o = oe[:, :ATT_HEAD_DIM] / oe[:, ATT_HEAD_DIM:]
        o_ref[q0:q0 + Q_BLOCK, hs] = (o * g_buf[q0:q0 + Q_BLOCK, hs]).astype(BF16)

    assert n_pieces == ATT_HEADS
    order = [(qb, hh) for qb in range(T // Q_BLOCK) for hh in range(ATT_HEADS)]
    for piece in range(n_pieces):
        project_piece(0, piece)
    s = scores(*order[0])
    for n, (qb, hh) in enumerate(order):
        if (qb + 1) * Q_BLOCK < T:
            project_piece(qb + 1, hh)
        s_next = scores(*order[n + 1]) if n + 1 < len(order) else None
        finish(qb, hh, s)
        s = s_next

    kt_buf[:, 0:PAST] = kt_buf[:, T:T + PAST]
    for hh in range(ATT_HEADS):
        vs = slice(hh * V_SLOT, hh * V_SLOT + ATT_HEAD_DIM)
        v_buf[0:PAST, vs] = v_buf[T:T + PAST, vs]


def _att_branch(x, norm_g, w_in, q_norm_g, k_norm_g, bias_table):
    B, S, _ = x.shape
    assert W_IN_ATT_COL % (2 * D_ATT) == 0
    qk_block = W_IN_ATT_COL // (2 * D_ATT)
    return pl.pallas_call(
        _att_kernel,
        grid=(B, S // ATT_SEQ_BLOCK),
        in_specs=[_token_spec(D_MODEL, ATT_SEQ_BLOCK), _const_spec((1, D_MODEL)),
                  _w_in_spec(2 * D_ATT, qk_block), _w_in_spec(2 * D_ATT, qk_block + 1),
                  _const_spec((1, ATT_HEAD_DIM)), _const_spec((1, ATT_HEAD_DIM)),
                  _const_spec((ATT_HEADS, BIAS_TABLE))],
        out_specs=_token_spec(D_ATT, ATT_SEQ_BLOCK),
        out_shape=jax.ShapeDtypeStruct((B, S, D_ATT), BF16),
        scratch_shapes=[pltpu.VMEM((ATT_SEQ_BLOCK, D_ATT), BF16),
                        pltpu.VMEM((D_ATT, PAST + ATT_SEQ_BLOCK), BF16),
                        pltpu.VMEM((PAST + ATT_SEQ_BLOCK, ATT_HEADS * V_SLOT), BF16),
                        pltpu.VMEM((ATT_SEQ_BLOCK, D_ATT), F32),
                        pltpu.VMEM((ATT_HEADS, Q_BLOCK, Q_KEYS), F32),
                        pltpu.VMEM((Q_BLOCK, D_MODEL), BF16)],
        compiler_params=_params(),
        name="att_branch",
    )(x, norm_g, w_in, w_in, q_norm_g, k_norm_g, bias_table)


def _bias_table(rel_bias):
    assert Q_KEYS - 1 - MAX_REL == BIAS_TABLE - (2 * MAX_REL + 1)
    return jnp.pad(rel_bias[:, ::-1], ((0, 0), (Q_KEYS - 1 - MAX_REL, 0)), mode="edge")


def _mem_kernel(x_ref, ng_ref, w_ref, qg_ref, mem_ref, mg_ref, wkv_ref, kg_ref, o_ref, kt_ref, v_ref):
    @pl.when(pl.program_id(1) == 0)
    def _():
        mn = _rms_rows(mem_ref[...], mg_ref[...]).astype(BF16)
        for hh in range(MEM_HEADS):
            c0 = hh * MEM_HEAD_DIM
            km = _dot(mn, wkv_ref[:, c0:c0 + MEM_HEAD_DIM])
            kt_ref[c0:c0 + MEM_HEAD_DIM, :] = _rms_rows(km, kg_ref[...]).T.astype(BF16)
            v_ref[:, c0:c0 + MEM_HEAD_DIM] = _dot(
                mn, wkv_ref[:, D_MEM + c0:D_MEM + c0 + MEM_HEAD_DIM]).astype(BF16)

    h = _rms_rows(x_ref[...], ng_ref[...]).astype(BF16)
    q_scale = MEM_HEAD_DIM ** -0.5 * LOG2E

    def head_slice(hh):
        return slice(hh * MEM_HEAD_DIM, (hh + 1) * MEM_HEAD_DIM)

    def q_head(hh):
        return (_rms_rows(_dot(h, w_ref[:, head_slice(hh)]), qg_ref[...]) * q_scale).astype(BF16)

    q = q_head(0)
    for hh in range(MEM_HEADS):
        hs = head_slice(hh)
        sg = jax.nn.silu(_dot(h, w_ref[:, D_MEM + hh * MEM_HEAD_DIM:D_MEM + (hh + 1) * MEM_HEAD_DIM]))
        q_next = q_head(hh + 1) if hh + 1 < MEM_HEADS else None
        s = _dot(q, kt_ref[hs, :])
        e = jnp.exp2(s - jnp.max(s, axis=-1, keepdims=True))
        l = jnp.sum(e, axis=-1, keepdims=True)
        o = _dot(e.astype(BF16), v_ref[:, hs]) / l
        o_ref[:, hs] = (o * sg).astype(BF16)
        q = q_next


def _mem_branch(x, norm_g, w_in, mem_q_norm_g, mem, mem_norm_g, w_mem_kv, mem_k_norm_g):
    B, S, _ = x.shape
    M = mem.shape[1]
    assert W_IN_MEM_COL % (2 * D_MEM) == 0
    mem_spec = pl.BlockSpec((None, M, D_MODEL), lambda b, s: (b, 0, 0))
    return pl.pallas_call(
        _mem_kernel,
        grid=(B, S // MEM_SEQ_BLOCK),
        in_specs=[_token_spec(D_MODEL, MEM_SEQ_BLOCK), _const_spec((1, D_MODEL)),
                  _w_in_spec(2 * D_MEM, W_IN_MEM_COL // (2 * D_MEM)), _const_spec((1, MEM_HEAD_DIM)),
                  mem_spec, _const_spec((1, D_MODEL)), _const_spec((D_MODEL, 2 * D_MEM)),
                  _const_spec((1, MEM_HEAD_DIM))],
        out_specs=_token_spec(D_MEM, MEM_SEQ_BLOCK),
        out_shape=jax.ShapeDtypeStruct((B, S, D_MEM), BF16),
        scratch_shapes=[pltpu.VMEM((D_MEM, M), BF16), pltpu.VMEM((M, D_MEM), BF16)],
        compiler_params=_params(),
        name="mem_branch",
    )(x, norm_g, w_in, mem_q_norm_g, mem, mem_norm_g, w_mem_kv, mem_k_norm_g)


def _rnn_out_kernel(x_ref, aa_ref, am_ref, ng_ref, p_ref, pt_ref, w_ref, cw_ref, cb_ref, wg_ref, bg_ref, lam_ref,
                    wmg0_ref, wmg1_ref, wmg2_ref, bm_ref, wpr_ref, wpa_ref, wpm_ref, wo_ref, o_ref,
                    xr_buf, a_buf, b_buf, sg_buf, gate_buf, g0_buf, zp_buf, carry_ref,
                    h_buf, hp_buf, y_buf, z_buf):
    R = RNN_ROWS
    chunks = [slice(c * COL_CHUNK, (c + 1) * COL_CHUNK) for c in range(D_MODEL // COL_CHUNK)]

    @pl.when(pl.program_id(1) == 0)
    def _():
        xr_buf[:, 0:CONV_HIST, :] = jnp.zeros((RNN_GROUPS, CONV_HIST, D_RNN), F32)
        carry_ref[...] = jnp.zeros(carry_ref.shape, F32)

    neg_log_a = LRU_C * jax.nn.softplus(-lam_ref[...])
    a_exp = -LOG2E * neg_log_a

    def merge_gate(n, cs):
        gs = slice(n * D_MODEL + cs.start, n * D_MODEL + cs.stop)
        return jax.nn.sigmoid(_dot(h_buf[...], (wmg0_ref, wmg1_ref, wmg2_ref)[n][:, cs]) + bm_ref[:, gs])

    for g in range(RNN_GROUPS):
        bs = slice(g * RNN_BATCH, (g + 1) * RNN_BATCH)
        xr_g = xr_buf.at[g]

        h_buf[...] = _rms_rows(x_ref[bs].reshape(R, D_MODEL), ng_ref[...]).astype(BF16)
        hp_buf[...] = _dot(p_ref[...], h_buf[...]).astype(BF16)
        for cs in chunks:
            xr_g[CONV_HIST:CONV_HIST + R, cs] = _dot(hp_buf[...], w_ref[:, cs])
            sg_buf[:, cs] = jax.nn.silu(_dot(hp_buf[...], w_ref[:, D_RNN + cs.start:D_RNN + cs.stop]))

        blocks_per_chunk = RNN_BLOCKS // len(chunks)
        for n in range(RNN_BLOCKS):
            cs = slice(n * RNN_BLOCK_DIM, (n + 1) * RNN_BLOCK_DIM)
            xc = cb_ref[:, cs]
            for j in range(CONV_WIDTH):
                xc = xc + cw_ref[j:j + 1, cs] * xr_g[j * RNN_BATCH:j * RNN_BATCH + R, cs]
            a_buf[:, cs] = xc
            gate_buf[:, 2 * n * RNN_BLOCK_DIM:2 * (n + 1) * RNN_BLOCK_DIM] = (
                _dot(xc.astype(BF16), wg_ref[n]) + bg_ref[n])
            if n % blocks_per_chunk == blocks_per_chunk - 1:
                cm = chunks[n // blocks_per_chunk]
                g0_buf[:, cm] = merge_gate(0, cm)
        xr_g[0:CONV_HIST, :] = xr_g[R:R + CONV_HIST, :]

        aa = aa_ref[bs].reshape(R, D_ATT)
        am = am_ref[bs].reshape(R, D_MEM)
        for cs in chunks:
            zp_buf[:, cs] = (merge_gate(1, cs) * _dot(aa, wpa_ref[:, cs])
                             + merge_gate(2, cs) * _dot(am, wpm_ref[:, cs]))

        for n in range(RNN_BLOCKS):
            cs = slice(n * RNN_BLOCK_DIM, (n + 1) * RNN_BLOCK_DIM)
            xc = a_buf[:, cs]
            r = jax.nn.sigmoid(gate_buf[:, 2 * n * RNN_BLOCK_DIM:(2 * n + 1) * RNN_BLOCK_DIM])
            i = jax.nn.sigmoid(gate_buf[:, (2 * n + 1) * RNN_BLOCK_DIM:(2 * n + 2) * RNN_BLOCK_DIM])
            a = jnp.exp2(r * a_exp[:, cs])
            one_minus_a2 = jnp.tanh(r * neg_log_a[:, cs]) * (1.0 + a * a)
            root = jnp.where(one_minus_a2 > 0.0, one_minus_a2 * lax.rsqrt(one_minus_a2), 0.0)
            a_buf[:, cs] = a
            b_buf[:, cs] = root * (i * xc)

        hc = carry_ref[g]
        for t in range(RNN_STEPS):
            rows = slice(t * RNN_BATCH, (t + 1) * RNN_BATCH)
            hc = a_buf[rows, :] * hc + b_buf[rows, :]
            b_buf[rows, :] = hc
        carry_ref[g] = hc

        for cs in chunks:
            y = (b_buf[:, cs] * sg_buf[:, cs]).astype(BF16)
            y_buf[:, cs] = _dot(pt_ref[...], y).astype(BF16)

        for cs in chunks:
            z_buf[:, cs] = (zp_buf[:, cs] + g0_buf[:, cs] * _dot(y_buf[...], wpr_ref[:, cs])).astype(BF16)
        for cs in chunks:
            out = x_ref[bs, :, cs].reshape(R, COL_CHUNK) + _dot(z_buf[...], wo_ref[:, cs])
            o_ref[bs, :, cs] = out.reshape(RNN_BATCH, RNN_STEPS, COL_CHUNK)


def _row_shuffle():
    out_row = np.arange(RNN_ROWS)
    t, b = out_row // RNN_BATCH, out_row % RNN_BATCH
    p = np.zeros((RNN_ROWS, RNN_ROWS), np.float32)
    p[out_row, b * RNN_STEPS + t] = 1.0
    return p


def _rnn_merge_out(x, a_att, a_mem, norm_g, w_in, conv_w, conv_b, w_gates, b_gates, lru_lambda,
                   b_merge, w_square):
    B, S, _ = x.shape
    p = _row_shuffle()
    assert W_IN_MERGE_COL % D_MODEL == 0
    merge_block = W_IN_MERGE_COL // D_MODEL
    blk = lambda width: pl.BlockSpec((RNN_GROUPS * RNN_BATCH, RNN_STEPS, width), lambda b, s: (b, s, 0))
    sq = lambda n: pl.BlockSpec((None, D_MODEL, D_MODEL), lambda *_: (n, 0, 0), pipeline_mode=pl.Buffered(1))
    plane =lambda dtype, width=D_RNN: pltpu.VMEM((RNN_ROWS, width), dtype)
    return pl.pallas_call(
        _rnn_out_kernel,
        grid=(B // (RNN_GROUPS * RNN_BATCH), S // RNN_STEPS),
        in_specs=[blk(D_MODEL), blk(D_ATT), blk(D_MEM), _const_spec((1, D_MODEL)),
                  _const_spec((RNN_ROWS, RNN_ROWS)), _const_spec((RNN_ROWS, RNN_ROWS)),
                  _w_in_spec(2 * D_RNN, 0),
                  _const_spec((CONV_WIDTH, D_RNN)), _const_spec((1, D_RNN)),
                  _const_spec((RNN_BLOCKS, RNN_BLOCK_DIM, 2 * RNN_BLOCK_DIM)),
                  _const_spec((RNN_BLOCKS, 1, 2 * RNN_BLOCK_DIM)), _const_spec((1, D_RNN)),
                  _w_in_spec(D_MODEL, merge_block), _w_in_spec(D_MODEL, merge_block + 1),
                  _w_in_spec(D_MODEL, merge_block + 2), _const_spec((1, N_BRANCHES * D_MODEL)),
                  sq(0), sq(1), sq(2), sq(3)],
        out_specs=blk(D_MODEL),
        out_shape=jax.ShapeDtypeStruct((B, S, D_MODEL), F32),
        scratch_shapes=[pltpu.VMEM((RNN_GROUPS, CONV_HIST + RNN_ROWS, D_RNN), F32),
                        plane(F32), plane(F32), plane(F32), plane(F32, 2 * D_RNN), plane(F32), plane(F32),
                        pltpu.VMEM((RNN_GROUPS, RNN_BATCH, D_RNN), F32),
                        plane(BF16), plane(BF16), plane(BF16), plane(BF16)],
        compiler_params=_params(),
        name="rnn_merge_out",
    )(x, a_att, a_mem, norm_g, jnp.asarray(p, BF16), jnp.asarray(p.T, BF16), w_in, conv_w, conv_b, w_gates,
      b_gates, lru_lambda, w_in, w_in, w_in, b_merge, w_square, w_square, w_square, w_square)


@jax.jit
def kernel(x, mem, norm_g, mem_norm_g, w_in, b_merge, conv_w, conv_b, lru_wa, lru_ba, lru_wx, lru_bx, lru_lambda, q_norm_g, k_norm_g, rel_bias, w_mem_kv, mem_q_norm_g, mem_k_norm_g, w_proj_rnn, w_proj_att, w_proj_mem, w_out):
    B, S, D = x.shape
    assert D == D_MODEL and S % ATT_SEQ_BLOCK == 0 and S % MEM_SEQ_BLOCK == 0 and w_in.shape[0] == 1
    assert B % (RNN_GROUPS * RNN_BATCH) == 0

    def row(p):
        return p[0].reshape(1, -1)

    w = w_in[0].astype(BF16)
    w_gates = jnp.concatenate([lru_wa[0], lru_wx[0]], axis=-1).astype(BF16)
    b_gates = jnp.concatenate([lru_ba[0].reshape(RNN_BLOCKS, 1, RNN_BLOCK_DIM),
                               lru_bx[0].reshape(RNN_BLOCKS, 1, RNN_BLOCK_DIM)], axis=-1)
    ng = row(norm_g)

    a_att = _att_branch(x, ng, w, row(q_norm_g), row(k_norm_g), _bias_table(rel_bias[0]))
    a_mem = _mem_branch(x, ng, w, row(mem_q_norm_g), mem, row(mem_norm_g), w_mem_kv[0].astype(BF16),
                        row(mem_k_norm_g))
    return _rnn_merge_out(x, a_att, a_mem, ng, w, conv_w[0], row(conv_b), w_gates, b_gates, row(lru_lambda),
                          row(b_merge),
                          jnp.stack([w_proj_rnn[0], w_proj_att[0], w_proj_mem[0], w_out[0]]).astype(BF16))
```
